```python
import functools
import jax, jax.numpy as jnp
from jax import lax
import numpy as np

D_MODEL = 1024
BATCH = 8
SEQ = 4096
DEPTH = 1
DEC_BATCH = 128
DEC_SEQ = 8
PAST_LEN = 8192
PAGE_SIZE = 128

HEAD_DIM = 64
RWKV_HEADS = 8
RWKV_WIDTH = RWKV_HEADS * HEAD_DIM
W_RANK = 64
A_RANK = 64
G_RANK = 128
RWKV_COLS = 3 * RWKV_WIDTH + W_RANK + A_RANK + G_RANK
RWKV_SPLITS = (RWKV_WIDTH, 2 * RWKV_WIDTH, 3 * RWKV_WIDTH, 3 * RWKV_WIDTH + W_RANK, 3 * RWKV_WIDTH + W_RANK + A_RANK)
GN_EPS = 64e-5
NSA_HEADS = 8
NSA_KV = 2
NSA_GROUP = NSA_HEADS // NSA_KV
NSA_WIDTH = NSA_HEADS * HEAD_DIM
KV_WIDTH = NSA_KV * HEAD_DIM
N_NSA_BRANCH = 3
NSA_COLS = NSA_WIDTH + N_NSA_BRANCH * 2 * KV_WIDTH + N_NSA_BRANCH * NSA_HEADS
CMP_BLOCK = 32
CMP_STRIDE = 16
CMP_RATIO = CMP_BLOCK // CMP_STRIDE
CMP_HIDDEN = 256
SEL_BLOCK = 64
N_SELECT = 16
WINDOW = 512
Q_BLOCK = 128
ROPE_THETA = 10000.0
IN_COLS = RWKV_COLS + NSA_COLS + 2 * D_MODEL
PEER_HEADS = 8
PEER_DK = 128
N_KEYS = 128
N_EXPERTS = N_KEYS * N_KEYS
PEER_TOPK = 16
PEER_BLOCK = 256
PLE_DIM = 256
RMS_EPS = 1e-6

kernel_name = 'rwkv7_nsa_peer_hybrid_step'


def rmsnorm(x, g):
    xf = x.astype(jnp.float32)
    y = xf * lax.rsqrt(jnp.mean(xf * xf, -1, keepdims=True) + RMS_EPS)
    return (y * g).astype(x.dtype)


def rope(x, pos):
    half = HEAD_DIM // 2
    inv = ROPE_THETA ** (-jnp.arange(half, dtype=jnp.float32) / half)
    ang = pos.astype(jnp.float32)[:, None] * inv
    cos, sin = jnp.cos(ang)[:, None, :], jnp.sin(ang)[:, None, :]
    xf = x.astype(jnp.float32)
    x1, x2 = xf[..., :half], xf[..., half:]
    return jnp.concatenate([x1 * cos - x2 * sin, x2 * cos + x1 * sin], -1).astype(x.dtype)


def masked_softmax(s, mask):
    s = jnp.where(mask, s.astype(jnp.float32), -jnp.inf)
    m = jnp.max(s, -1, keepdims=True)
    m = jnp.where(jnp.isfinite(m), m, 0.0)
    e = jnp.exp(s - m)
    return e / jnp.maximum(jnp.sum(e, -1, keepdims=True), 1e-30)


def rwkv_mix(zr, z_prev0, wkv0, lp):
    B, T, _ = zr.shape
    f32 = jnp.float32
    z_prev = jnp.concatenate([z_prev0[:, None, :].astype(zr.dtype), zr[:, :-1]], axis=1)
    xs = zr + (z_prev - zr) * lp['rwkv_mu']
    r, k, v, wl, al, gl = jnp.split(xs, RWKV_SPLITS, axis=-1)
    w = -jax.nn.softplus(-(lp['rwkv_w0'] + jnp.tanh(wl) @ lp['rwkv_w_up'])) - 0.5
    a = jax.nn.sigmoid(lp['rwkv_a0'] + al @ lp['rwkv_a_up'])
    g = jax.nn.sigmoid(gl) @ lp['rwkv_g_up']
    heads = lambda t: t.astype(f32).reshape(B, T, RWKV_HEADS, HEAD_DIM)
    kk = heads(k * lp['rwkv_k_k'])
    kk = kk / jnp.maximum(jnp.sqrt(jnp.sum(kk * kk, -1, keepdims=True)), 1e-12)
    k = k * (1.0 + (a - 1.0) * lp['rwkv_k_a'])
    r_h, k_h, v_h, a_h = heads(r), heads(k), heads(v), heads(a)
    decay = jnp.exp(-jnp.exp(heads(w)))

    def step(S, inp):
        r_t, k_t, v_t, d_t, kk_t, a_t = inp
        sa = jnp.einsum('bhvk,bhk->bhv', S, kk_t)
        S = S * d_t[:, :, None, :] - sa[..., None] * (kk_t * a_t)[:, :, None, :] + v_t[..., None] * k_t[:, :, None, :]
        return S, jnp.einsum('bhvk,bhk->bhv', S, r_t)

    seq = tuple(jnp.swapaxes(t, 0, 1) for t in (r_h, k_h, v_h, decay, kk, a_h))
    s_fin, out = lax.scan(step, wkv0.astype(f32), seq)
    out = jnp.swapaxes(out, 0, 1)
    mu = jnp.mean(out, -1, keepdims=True)
    var = jnp.mean(jnp.square(out - mu), -1, keepdims=True)
    on = ((out - mu) * lax.rsqrt(var + GN_EPS)).reshape(B, T, RWKV_WIDTH) * lp['rwkv_gn_w'] + lp['rwkv_gn_b']
    bonus = jnp.sum(r_h * k_h * lp['rwkv_r_k'], -1, keepdims=True) * v_h
    o = (on + bonus.reshape(B, T, RWKV_WIDTH)) * g
    return o.astype(zr.dtype), s_fin, zr[:, -1]


def nsa_project(zn, pos, lp):
    B, T, _ = zn.shape
    q = zn[..., :NSA_WIDTH].reshape(B, T, NSA_HEADS, HEAD_DIM)
    kv = zn[..., NSA_WIDTH:NSA_WIDTH + 6 * KV_WIDTH].reshape(B, T, N_NSA_BRANCH, 2, NSA_KV, HEAD_DIM)
    gates = jax.nn.sigmoid(zn[..., NSA_WIDTH + 6 * KV_WIDTH:].reshape(B, T, NSA_HEADS, N_NSA_BRANCH))
    qn = rmsnorm(q, lp['q_norm'])
    qr = rope(qn, pos)

    def prep(kvb, gk):
        return jnp.stack([rope(rmsnorm(kvb[:, :, 0], gk), pos), kvb[:, :, 1]], axis=2)

    return qn, qr, kv[:, :, 0], prep(kv[:, :, 1], lp['k_norm'][1]), prep(kv[:, :, 2], lp['k_norm'][2]), gates


def compress(kv_raw, lp):
    n_chunk = kv_raw.shape[0] // CMP_STRIDE
    nc = n_chunk - CMP_RATIO + 1
    chunks = kv_raw[:n_chunk * CMP_STRIDE].reshape(n_chunk, CMP_STRIDE, 2, NSA_KV, HEAD_DIM)
    w1 = lp['cmp_w1'].reshape(2, CMP_RATIO, CMP_STRIDE, HEAD_DIM, CMP_HIDDEN)
    part = jnp.einsum('csekd,ersdf->crekf', chunks, w1)
    hid = part[0:nc, 0]
    for r in range(1, CMP_RATIO):
        hid = hid + part[r:r + nc, r]
    pe_bias = jnp.einsum('esd,esdf->ef', lp['cmp_pe'], lp['cmp_w1'].reshape(2, CMP_BLOCK, HEAD_DIM, CMP_HIDDEN))
    hid = jax.nn.gelu(hid + pe_bias[None, :, None, :])
    out = jnp.einsum('nekf,efd->nekd', hid, lp['cmp_w2'])
    kc = rmsnorm(out[:, 0], lp['k_norm'][0])
    return kc, out[:, 1], jnp.arange(nc) * CMP_STRIDE + CMP_BLOCK - 1


def block_cover_matrix(nsb, nc):
    c0 = jnp.arange(nc) * CMP_STRIDE
    s0 = jnp.arange(nsb) * SEL_BLOCK
    cover = (c0[None, :] <= s0[:, None] + SEL_BLOCK - 1) & (c0[None, :] + CMP_BLOCK - 1 >= s0[:, None])
    return cover.astype(jnp.float32)


def nsa_core(qn, qr, q_pos, kc, vc, c_end, nsb, npb, gather_past, kv_new, kv_win, win_pos, gates):
    tq = qn.shape[0]
    scale = HEAD_DIM ** -0.5
    qn_g = qn.reshape(tq, NSA_KV, NSA_GROUP, HEAD_DIM)
    qr_g = qr.reshape(tq, NSA_KV, NSA_GROUP, HEAD_DIM)
    causal = q_pos[:, None] >= q_pos[None, :]
    p_c = masked_softmax(jnp.einsum('tkgd,nkd->kgtn', qn_g, kc) * scale, c_end[None, :] <= q_pos[:, None])
    o_c = jnp.einsum('kgtn,nkd->tkgd', p_c, vc)
    imp = jnp.einsum('kgtn,jn->ktj', p_c, block_cover_matrix(nsb, kc.shape[0]))
    blk = jnp.arange(nsb)[None, :]
    cur = (q_pos // SEL_BLOCK)[:, None]
    forced = (blk == cur) | (blk == cur - 1) | (blk == 0)
    imp = jnp.where(forced, jnp.inf, jnp.where(blk > cur, -jnp.inf, imp))
    n_sel = min(N_SELECT, nsb)
    sel = lax.top_k(imp, n_sel)[1]
    kv_p = gather_past(sel)
    m_len = n_sel * SEL_BLOCK
    k_p = kv_p[..., 0, :].reshape(NSA_KV, tq, m_len, HEAD_DIM)
    v_p = kv_p[..., 1, :].reshape(NSA_KV, tq, m_len, HEAD_DIM)
    s_sel = jnp.concatenate([jnp.einsum('tkgd,ktmd->kgtm', qr_g, k_p),
                             jnp.einsum('tkgd,skd->kgts', qr_g, kv_new[:, 0])], -1) * scale
    m_past = jnp.repeat(sel < npb, SEL_BLOCK, axis=-1)
    member = jnp.any(sel[..., None] == (q_pos // SEL_BLOCK), axis=-2)
    p_s = masked_softmax(s_sel, jnp.concatenate([m_past, member & causal], -1)[:, None])
    o_s = (jnp.einsum('kgtm,ktmd->tkgd', p_s[..., :m_len], v_p)
           + jnp.einsum('kgts,skd->tkgd', p_s[..., m_len:], kv_new[:, 1]))
    dist = q_pos[:, None] - win_pos[None, :]
    m_w = (dist >= 0) & (dist < WINDOW) & (win_pos >= 0)[None, :]
    p_w = masked_softmax(jnp.einsum('tkgd,skd->kgts', qr_g, kv_win[:, 0]) * scale, m_w)
    o_w = jnp.einsum('kgts,skd->tkgd', p_w, kv_win[:, 1])
    g = gates.reshape(tq, NSA_KV, NSA_GROUP, N_NSA_BRANCH)
    o = o_c * g[..., 0:1] + o_s * g[..., 1:2] + o_w * g[..., 2:3]
    return o.reshape(tq, NSA_WIDTH).astype(qn.dtype)


def nsa_prompt(qn, qr, kv_cmp, kv_sel, kv_win, gates, lp):
    S = qn.shape[1]
    nsb = S // SEL_BLOCK
    kv_idx = jnp.arange(NSA_KV)[:, None, None]

    def per_seq(args):
        qn1, qr1, kvc1, kvs1, kvw1, g1 = args
        kc, vc, c_end = compress(kvc1, lp)
        blocks = kvs1.reshape(nsb, SEL_BLOCK, 2, NSA_KV, HEAD_DIM)
        gather_past = lambda sel: blocks[sel, :, :, kv_idx]
        win_pad = jnp.pad(kvw1, ((WINDOW, 0), (0, 0), (0, 0), (0, 0)))

        def per_block(i):
            t0 = i * Q_BLOCK
            q_pos = t0 + jnp.arange(Q_BLOCK)
            sl = lambda a: lax.dynamic_slice_in_dim(a, t0, Q_BLOCK, 0)
            kw = lax.dynamic_slice_in_dim(win_pad, t0, WINDOW + Q_BLOCK, 0)
            w_pos = t0 - WINDOW + jnp.arange(WINDOW + Q_BLOCK)
            return nsa_core(sl(qn1), sl(qr1), q_pos, kc, vc, c_end, nsb, t0 // SEL_BLOCK,
                            gather_past, sl(kvs1), kw, w_pos, sl(g1))

        return lax.map(per_block, jnp.arange(S // Q_BLOCK)).reshape(S, NSA_WIDTH)

    return lax.map(per_seq, (qn, qr, kv_cmp, kv_sel, kv_win, gates))


def nsa_sample(qn, qr, kv_cmp, kv_sel, kv_win, gates, cache_cmp, cache_sel, page_table, state_win, lp):
    T = qn.shape[1]
    nsb = -(-(PAST_LEN + T) // SEL_BLOCK)
    npb = PAST_LEN // SEL_BLOCK
    bpp = PAGE_SIZE // SEL_BLOCK
    n_pages = PAST_LEN // PAGE_SIZE
    pool_blocks = cache_sel.reshape(cache_sel.shape[0] * bpp, SEL_BLOCK, 2, NSA_KV, HEAD_DIM)
    q_pos = PAST_LEN + jnp.arange(T)
    win_cache = state_win.shape[1]
    w_pos = PAST_LEN - win_cache + jnp.arange(win_cache + T)
    kv_idx = jnp.arange(NSA_KV)[:, None, None]

    def per_seq(args):
        qn1, qr1, kvc1, kvs1, kvw1, g1, pt, sw = args
        past_cmp = cache_cmp[pt].reshape(n_pages * PAGE_SIZE, 2, NSA_KV, HEAD_DIM).astype(kvc1.dtype)
        kc, vc, c_end = compress(jnp.concatenate([past_cmp, kvc1], 0), lp)

        def gather_past(sel):
            j = jnp.minimum(sel, npb - 1)
            return pool_blocks[pt[j // bpp] * bpp + j % bpp, :, :, kv_idx]

        kw = jnp.concatenate([sw.astype(kvw1.dtype), kvw1], 0)
        return nsa_core(qn1, qr1, q_pos, kc, vc, c_end, nsb, npb, gather_past, kvs1, kw, w_pos, g1)

    return lax.map(per_seq, (qn, qr, kv_cmp, kv_sel, kv_win, gates, page_table, state_win))


def peer_ffn(h, lp):
    lead = h.shape[:-1]
    ht = h.reshape(-1, D_MODEL)
    n = ht.shape[0]
    nb = -(-n // PEER_BLOCK)
    ht = jnp.pad(ht, ((0, nb * PEER_BLOCK - n), (0, 0))).reshape(nb, PEER_BLOCK, D_MODEL)

    def block(hb):
        q = (hb @ lp['peer_wq']).reshape(PEER_BLOCK, PEER_HEADS, 2, PEER_DK // 2)
        s = jnp.einsum('thcd,hcnd->thcn', q, lp['peer_keys']).astype(jnp.float32)
        sv, si = lax.top_k(s, PEER_TOPK)
        cand = (sv[:, :, 0, :, None] + sv[:, :, 1, None, :]).reshape(PEER_BLOCK, PEER_HEADS, PEER_TOPK * PEER_TOPK)
        cv, ci = lax.top_k(cand, PEER_TOPK)
        i1 = jnp.take_along_axis(si[:, :, 0], ci // PEER_TOPK, -1)
        i2 = jnp.take_along_axis(si[:, :, 1], ci % PEER_TOPK, -1)
        e = i1 * N_KEYS + i2
        gate = jax.nn.softmax(cv, -1)
        act = jax.nn.gelu(jnp.einsum('td,thkd->thk', hb, lp['peer_u'][e]).astype(jnp.float32))
        return jnp.einsum('thk,thkd->td', (gate * act).astype(hb.dtype), lp['peer_v'][e])

    out = lax.map(block, ht).reshape(nb * PEER_BLOCK, D_MODEL)[:n]
    return out.reshape(*lead, D_MODEL)


def layer_forward(x, p_l, pos, z_prev0, wkv0, attend, lp):
    h = rmsnorm(x, lp['attn_norm'])
    z = h @ lp['w_in']
    o_r, wkv, shift = rwkv_mix(z[..., :RWKV_COLS], z_prev0, wkv0, lp)
    qn, qr, kvc, kvs, kvw, gates = nsa_project(z[..., RWKV_COLS:RWKV_COLS + NSA_COLS], pos, lp)
    o_n = attend(qn, qr, kvc, kvs, kvw, gates)
    zg = z[..., RWKV_COLS + NSA_COLS:]
    merged = (jax.nn.sigmoid(zg[..., :D_MODEL]) * (o_r @ lp['w_rwkv_out'])
              + jax.nn.sigmoid(zg[..., D_MODEL:]) * (o_n @ lp['w_nsa_out']))
    x = x + merged @ lp['w_out']
    x = x + peer_ffn(rmsnorm(x, lp['ffn_norm']), lp)
    x = x + jax.nn.sigmoid(rmsnorm(x, lp['ple_norm']) @ lp['ple_w_gate']) * (p_l @ lp['ple_w_in'])
    return x, kvc, kvs, kvw, wkv, shift


def setup_inputs(seed: int = 0) -> dict:
    key = jax.random.key(seed)
    ks = iter(jax.random.split(key, 64))
    nrm = lambda shape, scale: jax.random.normal(next(ks), shape, jnp.float32) * scale
    gain = lambda shape: 1.0 + nrm(shape, 0.02)
    n_pages = PAST_LEN // PAGE_SIZE
    n_used = DEC_BATCH * n_pages
    n_phys = n_used + max(1, n_used // 4)
    win_cache = min(WINDOW, PAST_LEN)
    page_table = jax.random.permutation(next(ks), n_phys)[:n_used].reshape(DEC_BATCH, n_pages).astype(jnp.int32)
    L = DEPTH
    return {
        'x_prompt': nrm((BATCH, SEQ, D_MODEL), 1.0),
        'x_sample': nrm((DEC_BATCH, DEC_SEQ, D_MODEL), 1.0),
        'cache_cmp_kv': nrm((L, n_phys, PAGE_SIZE, 2, NSA_KV, HEAD_DIM), 1.0),
        'cache_sel_kv': nrm((L, n_phys, PAGE_SIZE, 2, NSA_KV, HEAD_DIM), 1.0),
        'state_win_kv': nrm((L, DEC_BATCH, win_cache, 2, NSA_KV, HEAD_DIM), 1.0),
        'state_wkv': nrm((L, DEC_BATCH, RWKV_HEADS, HEAD_DIM, HEAD_DIM), 0.3),
        'state_shift': nrm((L, DEC_BATCH, RWKV_COLS), 1.0),
        'page_table': page_table,
        'p_prompt': nrm((L, BATCH, SEQ, PLE_DIM), 1.0),
        'p_sample': nrm((L, DEC_BATCH, DEC_SEQ, PLE_DIM), 1.0),
        'attn_norm': gain((L, D_MODEL)),
        'w_in': nrm((L, D_MODEL, IN_COLS), D_MODEL ** -0.5),
        'rwkv_mu': jax.random.uniform(next(ks), (L, RWKV_COLS), jnp.float32),
        'rwkv_w0': jax.random.uniform(next(ks), (L, RWKV_WIDTH), jnp.float32, -3.0, 1.0),
        'rwkv_w_up': nrm((L, W_RANK, RWKV_WIDTH), 0.5 * W_RANK ** -0.5),
        'rwkv_a0': nrm((L, RWKV_WIDTH), 0.5),
        'rwkv_a_up': nrm((L, A_RANK, RWKV_WIDTH), 0.5 * A_RANK ** -0.5),
        'rwkv_g_up': nrm((L, G_RANK, RWKV_WIDTH), G_RANK ** -0.5),
        'rwkv_k_k': 0.85 + nrm((L, RWKV_WIDTH), 0.05),
        'rwkv_k_a': gain((L, RWKV_WIDTH)),
        'rwkv_r_k': nrm((L, RWKV_HEADS, HEAD_DIM), 0.1),
        'rwkv_gn_w': gain((L, RWKV_WIDTH)),
        'rwkv_gn_b': nrm((L, RWKV_WIDTH), 0.02),
        'w_rwkv_out': nrm((L, RWKV_WIDTH, D_MODEL), RWKV_WIDTH ** -0.5),
        'q_norm': gain((L, HEAD_DIM)),
        'k_norm': gain((L, N_NSA_BRANCH, HEAD_DIM)),
        'cmp_pe': nrm((L, 2, CMP_BLOCK, HEAD_DIM), 0.1),
        'cmp_w1': nrm((L, 2, CMP_BLOCK * HEAD_DIM, CMP_HIDDEN), (CMP_BLOCK * HEAD_DIM) ** -0.5),
        'cmp_w2': nrm((L, 2, CMP_HIDDEN, HEAD_DIM), CMP_HIDDEN ** -0.5),
        'w_nsa_out': nrm((L, NSA_WIDTH, D_MODEL), NSA_WIDTH ** -0.5),
        'w_out': nrm((L, D_MODEL, D_MODEL), D_MODEL ** -0.5),
        'ffn_norm': gain((L, D_MODEL)),
        'peer_wq': nrm((L, D_MODEL, PEER_HEADS * PEER_DK), D_MODEL ** -0.5),
        'peer_keys': nrm((L, PEER_HEADS, 2, N_KEYS, PEER_DK // 2), (PEER_DK // 2) ** -0.5),
        'peer_u': nrm((L, N_EXPERTS, D_MODEL), D_MODEL ** -0.5),
        'peer_v': nrm((L, N_EXPERTS, D_MODEL), 0.5),
        'ple_norm': gain((L, D_MODEL)),
        'ple_w_gate': nrm((L, D_MODEL, D_MODEL), D_MODEL ** -0.5),
        'ple_w_in': nrm((L, PLE_DIM, D_MODEL), PLE_DIM ** -0.5),
    }


def reference(x_prompt, x_sample, cache_cmp_kv, cache_sel_kv, state_win_kv, state_wkv, state_shift, page_table,
              p_prompt, p_sample, attn_norm, w_in, rwkv_mu, rwkv_w0, rwkv_w_up, rwkv_a0, rwkv_a_up, rwkv_g_up,
              rwkv_k_k, rwkv_k_a, rwkv_r_k, rwkv_gn_w, rwkv_gn_b, w_rwkv_out, q_norm, k_norm, cmp_pe, cmp_w1,
              cmp_w2, w_nsa_out, w_out, ffn_norm, peer_wq, peer_keys, peer_u, peer_v, ple_norm, ple_w_gate,
              ple_w_in):
    b_p, s_p = x_prompt.shape[:2]
    pos_p = jnp.arange(s_p)
    pos_s = PAST_LEN + jnp.arange(x_sample.shape[1])
    xp, xs = x_prompt, x_sample
    cmp_p, cmp_s, sel_p, sel_s, win_p, win_s = [], [], [], [], [], []
    wkv_p, wkv_s, sh_p, sh_s = [], [], [], []
    for l in range(DEPTH):
        lp = dict(attn_norm=attn_norm[l], w_in=w_in[l], rwkv_mu=rwkv_mu[l], rwkv_w0=rwkv_w0[l],
                  rwkv_w_up=rwkv_w_up[l], rwkv_a0=rwkv_a0[l], rwkv_a_up=rwkv_a_up[l], rwkv_g_up=rwkv_g_up[l],
                  rwkv_k_k=rwkv_k_k[l], rwkv_k_a=rwkv_k_a[l], rwkv_r_k=rwkv_r_k[l], rwkv_gn_w=rwkv_gn_w[l],
                  rwkv_gn_b=rwkv_gn_b[l], w_rwkv_out=w_rwkv_out[l], q_norm=q_norm[l], k_norm=k_norm[l],
                  cmp_pe=cmp_pe[l], cmp_w1=cmp_w1[l], cmp_w2=cmp_w2[l], w_nsa_out=w_nsa_out[l], w_out=w_out[l],
                  ffn_norm=ffn_norm[l], peer_wq=peer_wq[l], peer_keys=peer_keys[l], peer_u=peer_u[l],
                  peer_v=peer_v[l], ple_norm=ple_norm[l], ple_w_gate=ple_w_gate[l], ple_w_in=ple_w_in[l])
        attend_p = functools.partial(nsa_prompt, lp=lp)
        xp, kvc, kvs, kvw, wkv, sh = layer_forward(
            xp, p_prompt[l], pos_p, jnp.zeros((b_p, RWKV_COLS), xp.dtype),
            jnp.zeros((b_p, RWKV_HEADS, HEAD_DIM, HEAD_DIM), jnp.float32), attend_p, lp)
        cmp_p.append(kvc)
        sel_p.append(kvs)
        win_p.append(kvw[:, -min(WINDOW, s_p):])
        wkv_p.append(wkv)
        sh_p.append(sh)
        win_prev = state_win_kv[l]
        attend_s = functools.partial(nsa_sample, cache_cmp=cache_cmp_kv[l], cache_sel=cache_sel_kv[l],
                                     page_table=page_table, state_win=win_prev, lp=lp)
        xs, kvc, kvs, kvw, wkv, sh = layer_forward(xs, p_sample[l], pos_s, state_shift[l], state_wkv[l], attend_s, lp)
        cmp_s.append(kvc)
        sel_s.append(kvs)
        win_s.append(jnp.concatenate([win_prev.astype(kvw.dtype), kvw], 1)[:, -win_prev.shape[1]:])
        wkv_s.append(wkv)
        sh_s.append(sh)
    return (xp, xs, jnp.stack(cmp_p), jnp.stack(cmp_s), jnp.stack(sel_p), jnp.stack(sel_s),
            jnp.stack(win_p), jnp.stack(win_s), jnp.stack(wkv_p), jnp.stack(wkv_s), jnp.stack(sh_p), jnp.stack(sh_s))
```

```python
import functools
import jax, jax.numpy as jnp
from jax import lax
import numpy as np
from jax.experimental import pallas as pl
from jax.experimental.pallas import tpu as pltpu

LANES = 128
SUBLANES = 8
VMEM_TABLE_LIMIT = 48 * 1024 * 1024

D_MODEL = 1024
BATCH = 8
SEQ = 4096
DEPTH = 1
DEC_BATCH = 128
DEC_SEQ = 8
PAST_LEN = 8192
PAGE_SIZE = 128

HEAD_DIM = 64
RWKV_HEADS = 8
RWKV_WIDTH = RWKV_HEADS * HEAD_DIM
W_RANK = 64
A_RANK = 64
G_RANK = 128
RWKV_COLS = 3 * RWKV_WIDTH + W_RANK + A_RANK + G_RANK
RWKV_SPLITS = (RWKV_WIDTH, 2 * RWKV_WIDTH, 3 * RWKV_WIDTH, 3 * RWKV_WIDTH + W_RANK, 3 * RWKV_WIDTH + W_RANK + A_RANK)
GN_EPS = 64e-5
NSA_HEADS = 8
NSA_KV = 2
NSA_GROUP = NSA_HEADS // NSA_KV
NSA_WIDTH = NSA_HEADS * HEAD_DIM
KV_WIDTH = NSA_KV * HEAD_DIM
N_NSA_BRANCH = 3
NSA_COLS = NSA_WIDTH + N_NSA_BRANCH * 2 * KV_WIDTH + N_NSA_BRANCH * NSA_HEADS
CMP_BLOCK = 32
CMP_STRIDE = 16
CMP_RATIO = CMP_BLOCK // CMP_STRIDE
CMP_HIDDEN = 256
SEL_BLOCK = 64
N_SELECT = 16
WINDOW = 512
Q_BLOCK = 128
ROPE_THETA = 10000.0
IN_COLS = RWKV_COLS + NSA_COLS + 2 * D_MODEL
PEER_HEADS = 8
PEER_DK = 128
N_KEYS = 128
N_EXPERTS = N_KEYS * N_KEYS
PEER_TOPK = 16
PEER_BLOCK = 256
PLE_DIM = 256
RMS_EPS = 1e-6


def rmsnorm(x, g):
    xf = x.astype(jnp.float32)
    y = xf * lax.rsqrt(jnp.mean(xf * xf, -1, keepdims=True) + RMS_EPS)
    return (y * g).astype(x.dtype)


def rope(x, pos):
    half = HEAD_DIM // 2
    inv = ROPE_THETA ** (-jnp.arange(half, dtype=jnp.float32) / half)
    ang = pos.astype(jnp.float32)[:, None] * inv
    cos, sin = jnp.cos(ang)[:, None, :], jnp.sin(ang)[:, None, :]
    xf = x.astype(jnp.float32)
    x1, x2 = xf[..., :half], xf[..., half:]
    return jnp.concatenate([x1 * cos - x2 * sin, x2 * cos + x1 * sin], -1).astype(x.dtype)


def masked_softmax(s, mask):
    s = jnp.where(mask, s.astype(jnp.float32), -jnp.inf)
    m = jnp.max(s, -1, keepdims=True)
    m = jnp.where(jnp.isfinite(m), m, 0.0)
    e = jnp.exp(s - m)
    return e / jnp.maximum(jnp.sum(e, -1, keepdims=True), 1e-30)


WKV_GROUP = SUBLANES
WKV_CHUNK = 32
HEAD_PAIR = LANES // 64


def _wkv_kernel(kk_ref, kka_ref, d_ref, dr_ref, k_ref, v_ref, c1_ref, c2_ref, s0_ref,
                o_ref, sfin_ref, s_ref, mask_ref, *, chunk):
    f32, bf16 = jnp.float32, jnp.bfloat16
    hd = 64
    c = pl.program_id(1)
    n_col = s_ref.shape[2] // LANES

    @pl.when(c == 0)
    def _():
        s_ref[...] = s0_ref[0]
        lane = lax.broadcasted_iota(jnp.int32, mask_ref.shape, 2)
        row = lax.broadcasted_iota(jnp.int32, mask_ref.shape, 0)
        mask_ref[...] = ((lane % hd) == row).astype(f32)

    ones_bd = (lax.broadcasted_iota(jnp.int32, (LANES, LANES), 0) // hd
               == lax.broadcasted_iota(jnp.int32, (LANES, LANES), 1) // hd).astype(bf16)
    rows = hd * WKV_GROUP

    def seg(x3):
        x = x3.reshape(rows, LANES)
        hi = x.astype(bf16)
        lo = (x - hi.astype(f32)).astype(bf16)
        y = (jnp.dot(hi, ones_bd, preferred_element_type=f32)
             + jnp.dot(lo, ones_bd, preferred_element_type=f32))
        return y.reshape(hd, WKV_GROUP, LANES)

    def step(t, carry):
        kk, kka, d, dr = kk_ref[0, t], kka_ref[0, t], d_ref[0, t], dr_ref[0, t]
        k, v, c1, c2 = k_ref[0, t], v_ref[0, t], c1_ref[0, t], c2_ref[0, t]
        v_hi = v.astype(bf16).astype(f32)
        v_lo = v - v_hi
        for p in range(n_col):
            sl = slice(p * LANES, (p + 1) * LANES)
            m = mask_ref[...]
            s = s_ref[:, :, sl]
            sa = seg(s * kk[None, :, sl])
            tb = seg(s * dr[None, :, sl])
            vb = (jnp.dot((m * v_hi[None, :, sl]).reshape(rows, LANES).astype(bf16), ones_bd, preferred_element_type=f32)
                  + jnp.dot((m * v_lo[None, :, sl]).reshape(rows, LANES).astype(bf16), ones_bd, preferred_element_type=f32)
                  ).reshape(hd, WKV_GROUP, LANES)
            s_ref[:, :, sl] = s * d[None, :, sl] - sa * kka[None, :, sl] + vb * k[None, :, sl]
            ob = tb - sa * c1[None, :, sl] + vb * c2[None, :, sl]
            o_ref[0, t, :, sl] = jnp.sum(ob * m, axis=0)
        return carry

    lax.fori_loop(0, chunk, step, 0)

    @pl.when(c == pl.num_programs(1) - 1)
    def _():
        sfin_ref[0] = s_ref[...]


def wkv_scan(r_h, k_h, v_h, decay, kk, a_h, wkv0):
    B, T, H, N = r_h.shape
    width = H * N
    G = B // WKV_GROUP
    chunk = min(WKV_CHUNK, T)
    kka = kk * a_h
    dr = decay * r_h
    c1 = jnp.broadcast_to(jnp.sum(kka * r_h, -1, keepdims=True), r_h.shape)
    c2 = jnp.broadcast_to(jnp.sum(k_h * r_h, -1, keepdims=True), r_h.shape)
    tm = lambda t: t.reshape(G, WKV_GROUP, T, width).transpose(0, 2, 1, 3)
    s0 = wkv0.reshape(G, WKV_GROUP, H, N, N).transpose(0, 3, 1, 2, 4).reshape(G, N, WKV_GROUP, width)
    seq_spec = pl.BlockSpec((1, chunk, WKV_GROUP, width), lambda g, c: (g, c, 0, 0))
    st_spec = pl.BlockSpec((1, N, WKV_GROUP, width), lambda g, c: (g, 0, 0, 0))
    out, s_fin = pl.pallas_call(
        functools.partial(_wkv_kernel, chunk=chunk),
        grid=(G, T // chunk),
        in_specs=[seq_spec] * 8 + [st_spec],
        out_specs=[seq_spec, st_spec],
        out_shape=[jax.ShapeDtypeStruct((G, T, WKV_GROUP, width), jnp.float32),
                   jax.ShapeDtypeStruct((G, N, WKV_GROUP, width), jnp.float32)],
        scratch_shapes=[pltpu.VMEM((N, WKV_GROUP, width), jnp.float32),
                        pltpu.VMEM((N, WKV_GROUP, LANES), jnp.float32)],
        compiler_params=pltpu.CompilerParams(dimension_semantics=("arbitrary", "arbitrary")),
        name="wkv_scan",
    )(tm(kk), tm(kka), tm(decay), tm(dr), tm(k_h), tm(v_h), tm(c1), tm(c2), s0)
    out = out.transpose(0, 2, 1, 3).reshape(B, T, H, N)
    s_fin = s_fin.reshape(G, N, WKV_GROUP, H, N).transpose(0, 2, 3, 1, 4).reshape(B, H, N, N)
    return out, s_fin


def rwkv_mix(zr, z_prev0, wkv0, lp):
    B, T, _ = zr.shape
    f32 = jnp.float32
    z_prev = jnp.concatenate([z_prev0[:, None, :].astype(zr.dtype), zr[:, :-1]], axis=1)
    xs = zr + (z_prev - zr) * lp['rwkv_mu']
    r, k, v, wl, al, gl = jnp.split(xs, RWKV_SPLITS, axis=-1)
    w = -jax.nn.softplus(-(lp['rwkv_w0'] + jnp.tanh(wl) @ lp['rwkv_w_up'])) - 0.5
    a = jax.nn.sigmoid(lp['rwkv_a0'] + al @ lp['rwkv_a_up'])
    g = jax.nn.sigmoid(gl) @ lp['rwkv_g_up']
    heads = lambda t: t.astype(f32).reshape(B, T, RWKV_HEADS, HEAD_DIM)
    kk = heads(k * lp['rwkv_k_k'])
    kk = kk / jnp.maximum(jnp.sqrt(jnp.sum(kk * kk, -1, keepdims=True)), 1e-12)
    k = k * (1.0 + (a - 1.0) * lp['rwkv_k_a'])
    r_h, k_h, v_h, a_h = heads(r), heads(k), heads(v), heads(a)
    decay = jnp.exp(-jnp.exp(heads(w)))

    out, s_fin = wkv_scan(r_h, k_h, v_h, decay, kk, a_h, wkv0.astype(f32))
    mu = jnp.mean(out, -1, keepdims=True)
    var = jnp.mean(jnp.square(out - mu), -1, keepdims=True)
    on = ((out - mu) * lax.rsqrt(var + GN_EPS)).reshape(B, T, RWKV_WIDTH) * lp['rwkv_gn_w'] + lp['rwkv_gn_b']
    bonus = jnp.sum(r_h * k_h * lp['rwkv_r_k'], -1, keepdims=True) * v_h
    o = (on + bonus.reshape(B, T, RWKV_WIDTH)) * g
    return o.astype(zr.dtype), s_fin, zr[:, -1]


def nsa_project(zn, pos, lp):
    B, T, _ = zn.shape
    q = zn[..., :NSA_WIDTH].reshape(B, T, NSA_HEADS, HEAD_DIM)
    kv = zn[..., NSA_WIDTH:NSA_WIDTH + 6 * KV_WIDTH].reshape(B, T, N_NSA_BRANCH, 2, NSA_KV, HEAD_DIM)
    gates = jax.nn.sigmoid(zn[..., NSA_WIDTH + 6 * KV_WIDTH:].reshape(B, T, NSA_HEADS, N_NSA_BRANCH))
    qn = rmsnorm(q, lp['q_norm'])
    qr = rope(qn, pos)

    def prep(kvb, gk):
        return jnp.stack([rope(rmsnorm(kvb[:, :, 0], gk), pos), kvb[:, :, 1]], axis=2)

    return qn, qr, kv[:, :, 0], prep(kv[:, :, 1], lp['k_norm'][1]), prep(kv[:, :, 2], lp['k_norm'][2]), gates


def compress(kv_raw, lp):
    n_chunk = kv_raw.shape[0] // CMP_STRIDE
    nc = n_chunk - CMP_RATIO + 1
    chunks = kv_raw[:n_chunk * CMP_STRIDE].reshape(n_chunk, CMP_STRIDE, 2, NSA_KV, HEAD_DIM)
    w1 = lp['cmp_w1'].reshape(2, CMP_RATIO, CMP_STRIDE, HEAD_DIM, CMP_HIDDEN)
    part = jnp.einsum('csekd,ersdf->crekf', chunks, w1)
    hid = part[0:nc, 0]
    for r in range(1, CMP_RATIO):
        hid = hid + part[r:r + nc, r]
    pe_bias = jnp.einsum('esd,esdf->ef', lp['cmp_pe'], lp['cmp_w1'].reshape(2, CMP_BLOCK, HEAD_DIM, CMP_HIDDEN))
    hid = jax.nn.gelu(hid + pe_bias[None, :, None, :])
    out = jnp.einsum('nekf,efd->nekd', hid, lp['cmp_w2'])
    kc = rmsnorm(out[:, 0], lp['k_norm'][0])
    return kc, out[:, 1], jnp.arange(nc) * CMP_STRIDE + CMP_BLOCK - 1


def block_cover_matrix(nsb, nc):
    c0 = jnp.arange(nc) * CMP_STRIDE
    s0 = jnp.arange(nsb) * SEL_BLOCK
    cover = (c0[None, :] <= s0[:, None] + SEL_BLOCK - 1) & (c0[None, :] + CMP_BLOCK - 1 >= s0[:, None])
    return cover.astype(jnp.float32)


def nsa_core(qn, qr, q_pos, kc, vc, c_end, nsb, npb, gather_past, kv_new, kv_win, win_pos, gates):
    tq = qn.shape[0]
    scale = HEAD_DIM ** -0.5
    qn_g = qn.reshape(tq, NSA_KV, NSA_GROUP, HEAD_DIM)
    qr_g = qr.reshape(tq, NSA_KV, NSA_GROUP, HEAD_DIM)
    causal = q_pos[:, None] >= q_pos[None, :]
    p_c = masked_softmax(jnp.einsum('tkgd,nkd->kgtn', qn_g, kc) * scale, c_end[None, :] <= q_pos[:, None])
    o_c = jnp.einsum('kgtn,nkd->tkgd', p_c, vc)
    imp = jnp.einsum('kgtn,jn->ktj', p_c, block_cover_matrix(nsb, kc.shape[0]))
    blk = jnp.arange(nsb)[None, :]
    cur = (q_pos // SEL_BLOCK)[:, None]
    forced = (blk == cur) | (blk == cur - 1) | (blk == 0)
    imp = jnp.where(forced, jnp.inf, jnp.where(blk > cur, -jnp.inf, imp))
    n_sel = min(N_SELECT, nsb)
    sel = lax.top_k(imp, n_sel)[1]
    kv_p = gather_past(sel)
    m_len = n_sel * SEL_BLOCK
    k_p = kv_p[..., 0, :].reshape(NSA_KV, tq, m_len, HEAD_DIM)
    v_p = kv_p[..., 1, :].reshape(NSA_KV, tq, m_len, HEAD_DIM)
    s_sel = jnp.concatenate([jnp.einsum('tkgd,ktmd->kgtm', qr_g, k_p),
                             jnp.einsum('tkgd,skd->kgts', qr_g, kv_new[:, 0])], -1) * scale
    m_past = jnp.repeat(sel < npb, SEL_BLOCK, axis=-1)
    member = jnp.any(sel[..., None] == (q_pos // SEL_BLOCK), axis=-2)
    p_s = masked_softmax(s_sel, jnp.concatenate([m_past, member & causal], -1)[:, None])
    o_s = (jnp.einsum('kgtm,ktmd->tkgd', p_s[..., :m_len], v_p)
           + jnp.einsum('kgts,skd->tkgd', p_s[..., m_len:], kv_new[:, 1]))
    dist = q_pos[:, None] - win_pos[None, :]
    m_w = (dist >= 0) & (dist < WINDOW) & (win_pos >= 0)[None, :]
    p_w = masked_softmax(jnp.einsum('tkgd,skd->kgts', qr_g, kv_win[:, 0]) * scale, m_w)
    o_w = jnp.einsum('kgts,skd->tkgd', p_w, kv_win[:, 1])
    g = gates.reshape(tq, NSA_KV, NSA_GROUP, N_NSA_BRANCH)
    o = o_c * g[..., 0:1] + o_s * g[..., 1:2] + o_w * g[..., 2:3]
    return o.reshape(tq, NSA_WIDTH).astype(qn.dtype)


def nsa_prompt(qn, qr, kv_cmp, kv_sel, kv_win, gates, lp):
    S = qn.shape[1]
    nsb = S // SEL_BLOCK
    kv_idx = jnp.arange(NSA_KV)[:, None, None]

    def per_seq(args):
        qn1, qr1, kvc1, kvs1, kvw1, g1 = args
        kc, vc, c_end = compress(kvc1, lp)
        blocks = kvs1.reshape(nsb, SEL_BLOCK, 2, NSA_KV, HEAD_DIM)
        gather_past = lambda sel: blocks[sel, :, :, kv_idx]
        win_pad = jnp.pad(kvw1, ((WINDOW, 0), (0, 0), (0, 0), (0, 0)))

        def per_block(i):
            t0 = i * Q_BLOCK
            q_pos = t0 + jnp.arange(Q_BLOCK)
            sl = lambda a: lax.dynamic_slice_in_dim(a, t0, Q_BLOCK, 0)
            kw = lax.dynamic_slice_in_dim(win_pad, t0, WINDOW + Q_BLOCK, 0)
            w_pos = t0 - WINDOW + jnp.arange(WINDOW + Q_BLOCK)
            return nsa_core(sl(qn1), sl(qr1), q_pos, kc, vc, c_end, nsb, t0 // SEL_BLOCK,
                            gather_past, sl(kvs1), kw, w_pos, sl(g1))

        return lax.map(per_block, jnp.arange(S // Q_BLOCK)).reshape(S, NSA_WIDTH)

    return lax.map(per_seq, (qn, qr, kv_cmp, kv_sel, kv_win, gates))


def nsa_sample(qn, qr, kv_cmp, kv_sel, kv_win, gates, cache_cmp, cache_sel, page_table, state_win, lp):
    T = qn.shape[1]
    nsb = -(-(PAST_LEN + T) // SEL_BLOCK)
    npb = PAST_LEN // SEL_BLOCK
    bpp = PAGE_SIZE // SEL_BLOCK
    n_pages = PAST_LEN // PAGE_SIZE
    pool_blocks = cache_sel.reshape(cache_sel.shape[0] * bpp, SEL_BLOCK, 2, NSA_KV, HEAD_DIM)
    q_pos = PAST_LEN + jnp.arange(T)
    win_cache = state_win.shape[1]
    w_pos = PAST_LEN - win_cache + jnp.arange(win_cache + T)
    kv_idx = jnp.arange(NSA_KV)[:, None, None]

    def per_seq(args):
        qn1, qr1, kvc1, kvs1, kvw1, g1, pt, sw = args
        past_cmp = cache_cmp[pt].reshape(n_pages * PAGE_SIZE, 2, NSA_KV, HEAD_DIM).astype(kvc1.dtype)
        kc, vc, c_end = compress(jnp.concatenate([past_cmp, kvc1], 0), lp)

        def gather_past(sel):
            j = jnp.minimum(sel, npb - 1)
            return pool_blocks[pt[j // bpp] * bpp + j % bpp, :, :, kv_idx]

        kw = jnp.concatenate([sw.astype(kvw1.dtype), kvw1], 0)
        return nsa_core(qn1, qr1, q_pos, kc, vc, c_end, nsb, npb, gather_past, kvs1, kw, w_pos, g1)

    return lax.map(per_seq, (qn, qr, kv_cmp, kv_sel, kv_win, gates, page_table, state_win))


PEER_SLOTS = PEER_HEADS * PEER_TOPK
PEER_TOK = 128
HALF_D = D_MODEL // 2
ROW_SUB = HALF_D // LANES


def pack_expert_table(tab):
    bits = lax.bitcast_convert_type(tab.astype(jnp.bfloat16), jnp.uint16).astype(jnp.uint32)
    return ((bits[:, HALF_D:] << 16) | bits[:, :HALF_D]).reshape(tab.shape[0], ROW_SUB, LANES)


def _unpack_row(row):
    lo = lax.bitcast_convert_type(row << 16, jnp.float32)
    hi = lax.bitcast_convert_type(row & jnp.uint32(0xFFFF0000), jnp.float32)
    return lo, hi


def _load_table_and_indices(idx_hbm, tab_hbm, tab_vmem, idx_smem, sem):
    i = pl.program_id(0)

    @pl.when(i == 0)
    def _():
        cp = pltpu.make_async_copy(tab_hbm, tab_vmem, sem.at[0])
        cp.start()
        cp.wait()

    n = idx_smem.shape[0]
    cp = pltpu.make_async_copy(idx_hbm.at[pl.ds(i * n, n)], idx_smem, sem.at[1])
    cp.start()
    cp.wait()


def _eye():
    return (lax.broadcasted_iota(jnp.int32, (PEER_SLOTS, LANES), 0)
            == lax.broadcasted_iota(jnp.int32, (PEER_SLOTS, LANES), 1))


def _peer_act_kernel(idx_hbm, h_ref, gate_ref, tab_hbm, w_ref, tab_vmem, idx_smem, ps_ref, a_ref, sem):
    _load_table_and_indices(idx_hbm, tab_hbm, tab_vmem, idx_smem, sem)
    eye = _eye()

    def tok(t, carry):
        h_lo = h_ref[t, 0:ROW_SUB, :]
        h_hi = h_ref[t, ROW_SUB:2 * ROW_SUB, :]
        for j in range(PEER_SLOTS):
            lo, hi = _unpack_row(tab_vmem[idx_smem[t * PEER_SLOTS + j]])
            ps_ref[j * ROW_SUB:(j + 1) * ROW_SUB, :] = lo * h_lo + hi * h_hi
        q = ps_ref[pl.ds(0, PEER_SLOTS, stride=ROW_SUB), :]
        for s in range(1, ROW_SUB):
            q = q + ps_ref[pl.ds(s, PEER_SLOTS, stride=ROW_SUB), :]
        a_col = jnp.sum(q, axis=-1, keepdims=True)
        a_ref[pl.ds(t, 1), :] = jnp.sum(jnp.where(eye, a_col, 0.0), axis=0, keepdims=True)
        return carry

    lax.fori_loop(0, h_ref.shape[0], tok, 0)
    w_ref[...] = gate_ref[...] * jax.nn.gelu(a_ref[...])


def _peer_out_kernel(idx_hbm, w_ref, tab_hbm, o_ref, tab_vmem, idx_smem, wb_ref, sem):
    _load_table_and_indices(idx_hbm, tab_hbm, tab_vmem, idx_smem, sem)
    eye = _eye()
    n_acc = 4

    def tok(t, carry):
        w_row = w_ref[pl.ds(t, 1), :]
        w_col = jnp.sum(jnp.where(eye, w_row, 0.0), axis=-1, keepdims=True)
        wb_ref[...] = jnp.broadcast_to(w_col, (PEER_SLOTS, LANES))
        acc_lo = [jnp.zeros((ROW_SUB, LANES), jnp.float32) for _ in range(n_acc)]
        acc_hi = [jnp.zeros((ROW_SUB, LANES), jnp.float32) for _ in range(n_acc)]
        for j in range(PEER_SLOTS):
            lo, hi = _unpack_row(tab_vmem[idx_smem[t * PEER_SLOTS + j]])
            wj = wb_ref[j:j + 1, :]
            acc_lo[j % n_acc] = acc_lo[j % n_acc] + wj * lo
            acc_hi[j % n_acc] = acc_hi[j % n_acc] + wj * hi
        o_ref[t, 0:ROW_SUB, :] = (acc_lo[0] + acc_lo[1]) + (acc_lo[2] + acc_lo[3])
        o_ref[t, ROW_SUB:2 * ROW_SUB, :] = (acc_hi[0] + acc_hi[1]) + (acc_hi[2] + acc_hi[3])
        return carry

    lax.fori_loop(0, w_ref.shape[0], tok, 0)


def peer_experts(ht, e, gate, u_packed, v_packed):
    n = ht.shape[0]
    n_exp = u_packed.shape[0]
    tb = min(PEER_TOK, n)
    assert n % tb == 0 and tb % SUBLANES == 0
    idx =e.reshape(n * PEER_SLOTS)
    h3 = ht.reshape(n, 2 * ROW_SUB, LANES)
    any_spec = pl.BlockSpec(memory_space=pl.ANY)
    tok_spec = pl.BlockSpec((tb, PEER_SLOTS), lambda i: (i, 0))
    row_spec = pl.BlockSpec((tb, 2 * ROW_SUB, LANES), lambda i: (i, 0, 0))
    params = pltpu.CompilerParams(dimension_semantics=("arbitrary",), vmem_limit_bytes=VMEM_TABLE_LIMIT)
    table = pltpu.VMEM((n_exp, ROW_SUB, LANES), jnp.uint32)
    idx_smem = pltpu.SMEM((tb * PEER_SLOTS,), jnp.int32)
    w = pl.pallas_call(
        _peer_act_kernel,
        grid=(n // tb,),
        in_specs=[any_spec, row_spec, tok_spec, any_spec],
        out_specs=tok_spec,
        out_shape=jax.ShapeDtypeStruct((n, PEER_SLOTS), jnp.float32),
        scratch_shapes=[table, idx_smem, pltpu.VMEM((PEER_SLOTS * ROW_SUB, LANES), jnp.float32),
                        pltpu.VMEM((tb, PEER_SLOTS), jnp.float32), pltpu.SemaphoreType.DMA((2,))],
        compiler_params=params,
        name="peer_act",
    )(idx, h3, gate, u_packed)
    out = pl.pallas_call(
        _peer_out_kernel,
        grid=(n // tb,),
        in_specs=[any_spec, tok_spec, any_spec],
        out_specs=row_spec,
        out_shape=jax.ShapeDtypeStruct((n, 2 * ROW_SUB, LANES), jnp.float32),
        scratch_shapes=[table, idx_smem, pltpu.VMEM((PEER_SLOTS, LANES), jnp.float32),
                        pltpu.SemaphoreType.DMA((2,))],
        compiler_params=params,
        name="peer_out",
    )(idx, w, v_packed)
    return out.reshape(n, D_MODEL)


def peer_ffn(h, lp):
    lead = h.shape[:-1]
    ht = h.reshape(-1, D_MODEL)
    n = ht.shape[0]
    q = (ht @ lp['peer_wq']).reshape(n, PEER_HEADS, 2, PEER_DK // 2)
    s = jnp.einsum('thcd,hcnd->thcn', q, lp['peer_keys']).astype(jnp.float32)
    sv, si = lax.top_k(s, PEER_TOPK)
    cand = (sv[:, :, 0, :, None] + sv[:, :, 1, None, :]).reshape(n, PEER_HEADS, PEER_TOPK * PEER_TOPK)
    cv, ci = lax.top_k(cand, PEER_TOPK)
    i1 = jnp.take_along_axis(si[:, :, 0], ci // PEER_TOPK, -1)
    i2 = jnp.take_along_axis(si[:, :, 1], ci % PEER_TOPK, -1)
    e = (i1 * N_KEYS + i2).reshape(n, PEER_SLOTS).astype(jnp.int32)
    gate = jax.nn.softmax(cv, -1).reshape(n, PEER_SLOTS)
    out = peer_experts(ht, e, gate, lp['peer_u_packed'], lp['peer_v_packed'])
    return out.reshape(*lead, D_MODEL)


def layer_forward(x, p_l, pos, z_prev0, wkv0, attend, lp):
    h = rmsnorm(x, lp['attn_norm'])
    z = h @ lp['w_in']
    o_r, wkv, shift = rwkv_mix(z[..., :RWKV_COLS], z_prev0, wkv0, lp)
    qn, qr, kvc, kvs, kvw, gates = nsa_project(z[..., RWKV_COLS:RWKV_COLS + NSA_COLS], pos, lp)
    o_n = attend(qn, qr, kvc, kvs, kvw, gates)
    zg = z[..., RWKV_COLS + NSA_COLS:]
    merged = (jax.nn.sigmoid(zg[..., :D_MODEL]) * (o_r @ lp['w_rwkv_out'])
              + jax.nn.sigmoid(zg[..., D_MODEL:]) * (o_n @ lp['w_nsa_out']))
    x = x + merged @ lp['w_out']
    x = x + peer_ffn(rmsnorm(x, lp['ffn_norm']), lp)
    x = x + jax.nn.sigmoid(rmsnorm(x, lp['ple_norm']) @ lp['ple_w_gate']) * (p_l @ lp['ple_w_in'])
    return x, kvc, kvs, kvw, wkv, shift


def kernel(x_prompt, x_sample, cache_cmp_kv, cache_sel_kv, state_win_kv, state_wkv, state_shift, page_table,
           p_prompt, p_sample, attn_norm, w_in, rwkv_mu, rwkv_w0, rwkv_w_up, rwkv_a0, rwkv_a_up, rwkv_g_up,
           rwkv_k_k, rwkv_k_a, rwkv_r_k, rwkv_gn_w, rwkv_gn_b, w_rwkv_out, q_norm, k_norm, cmp_pe, cmp_w1,
           cmp_w2, w_nsa_out, w_out, ffn_norm, peer_wq, peer_keys, peer_u, peer_v, ple_norm, ple_w_gate,
           ple_w_in):
    b_p, s_p = x_prompt.shape[:2]
    pos_p = jnp.arange(s_p)
    pos_s = PAST_LEN + jnp.arange(x_sample.shape[1])
    xp, xs = x_prompt, x_sample
    l = 0
    lp = dict(attn_norm=attn_norm[l], w_in=w_in[l], rwkv_mu=rwkv_mu[l], rwkv_w0=rwkv_w0[l],
              rwkv_w_up=rwkv_w_up[l], rwkv_a0=rwkv_a0[l], rwkv_a_up=rwkv_a_up[l], rwkv_g_up=rwkv_g_up[l],
              rwkv_k_k=rwkv_k_k[l], rwkv_k_a=rwkv_k_a[l], rwkv_r_k=rwkv_r_k[l], rwkv_gn_w=rwkv_gn_w[l],
              rwkv_gn_b=rwkv_gn_b[l], w_rwkv_out=w_rwkv_out[l], q_norm=q_norm[l], k_norm=k_norm[l],
              cmp_pe=cmp_pe[l], cmp_w1=cmp_w1[l], cmp_w2=cmp_w2[l], w_nsa_out=w_nsa_out[l], w_out=w_out[l],
              ffn_norm=ffn_norm[l], peer_wq=peer_wq[l], peer_keys=peer_keys[l], peer_u=peer_u[l],
              peer_v=peer_v[l], ple_norm=ple_norm[l], ple_w_gate=ple_w_gate[l], ple_w_in=ple_w_in[l],
              peer_u_packed=pack_expert_table(peer_u[l]), peer_v_packed=pack_expert_table(peer_v[l]))
    attend_p = functools.partial(nsa_prompt, lp=lp)
    xp, kvc_p, kvs_p, kvw_p, wkv_p, sh_p = layer_forward(
        xp, p_prompt[l], pos_p, jnp.zeros((b_p, RWKV_COLS), xp.dtype),
        jnp.zeros((b_p, RWKV_HEADS, HEAD_DIM, HEAD_DIM), jnp.float32), attend_p, lp)
    win_prev = state_win_kv[l]
    attend_s = functools.partial(nsa_sample, cache_cmp=cache_cmp_kv[l], cache_sel=cache_sel_kv[l],
                                 page_table=page_table, state_win=win_prev, lp=lp)
    xs, kvc_s, kvs_s, kvw_s, wkv_s, sh_s = layer_forward(xs, p_sample[l], pos_s, state_shift[l], state_wkv[l], attend_s, lp)
    win_s = jnp.concatenate([win_prev.astype(kvw_s.dtype), kvw_s], 1)[:, -win_prev.shape[1]:]
    st = lambda a: a[None]
    return (xp, xs, st(kvc_p), st(kvc_s), st(kvs_p), st(kvs_s),
            st(kvw_p[:, -min(WINDOW, s_p):]), st(win_s), st(wkv_p), st(wkv_s), st(sh_p), st(sh_s))
```

```python
import functools
import jax, jax.numpy as jnp
from jax import lax
import numpy as np
from jax.experimental import pallas as pl
from jax.experimental.pallas import tpu as pltpu

LANES = 128
SUBLANES = 8
VMEM_TABLE_LIMIT = 48 * 1024 * 1024

D_MODEL = 1024
BATCH = 8
SEQ = 4096
DEPTH = 1
DEC_BATCH = 128
DEC_SEQ = 8
PAST_LEN = 8192
PAGE_SIZE = 128

HEAD_DIM = 64
RWKV_HEADS = 8
RWKV_WIDTH = RWKV_HEADS * HEAD_DIM
W_RANK = 64
A_RANK = 64
G_RANK = 128
RWKV_COLS = 3 * RWKV_WIDTH + W_RANK + A_RANK + G_RANK
RWKV_SPLITS = (RWKV_WIDTH, 2 * RWKV_WIDTH, 3 * RWKV_WIDTH, 3 * RWKV_WIDTH + W_RANK, 3 * RWKV_WIDTH + W_RANK + A_RANK)
GN_EPS = 64e-5
NSA_HEADS = 8
NSA_KV = 2
NSA_GROUP = NSA_HEADS // NSA_KV
NSA_WIDTH = NSA_HEADS * HEAD_DIM
KV_WIDTH = NSA_KV * HEAD_DIM
N_NSA_BRANCH = 3
NSA_COLS = NSA_WIDTH + N_NSA_BRANCH * 2 * KV_WIDTH + N_NSA_BRANCH * NSA_HEADS
CMP_BLOCK = 32
CMP_STRIDE = 16
CMP_RATIO = CMP_BLOCK // CMP_STRIDE
CMP_HIDDEN = 256
SEL_BLOCK = 64
N_SELECT = 16
WINDOW = 512
Q_BLOCK = 128
ROPE_THETA = 10000.0
IN_COLS = RWKV_COLS + NSA_COLS + 2 * D_MODEL
PEER_HEADS = 8
PEER_DK = 128
N_KEYS = 128
N_EXPERTS = N_KEYS * N_KEYS
PEER_TOPK = 16
PEER_BLOCK = 256
PLE_DIM = 256
RMS_EPS = 1e-6


def rmsnorm(x, g):
    xf = x.astype(jnp.float32)
    y = xf * lax.rsqrt(jnp.mean(xf * xf, -1, keepdims=True) + RMS_EPS)
    return (y * g).astype(x.dtype)


def rope(x, pos):
    half = HEAD_DIM // 2
    inv = ROPE_THETA ** (-jnp.arange(half, dtype=jnp.float32) / half)
    ang = pos.astype(jnp.float32)[:, None] * inv
    cos, sin = jnp.cos(ang)[:, None, :], jnp.sin(ang)[:, None, :]
    xf = x.astype(jnp.float32)
    x1, x2 = xf[..., :half], xf[..., half:]
    return jnp.concatenate([x1 * cos - x2 * sin, x2 * cos + x1 * sin], -1).astype(x.dtype)


def masked_softmax(s, mask):
    s = jnp.where(mask, s.astype(jnp.float32), -jnp.inf)
    m = jnp.max(s, -1, keepdims=True)
    m = jnp.where(jnp.isfinite(m), m, 0.0)
    e = jnp.exp(s - m)
    return e / jnp.maximum(jnp.sum(e, -1, keepdims=True), 1e-30)


WKV_GROUP = SUBLANES
WKV_CHUNK = 32
HEAD_PAIR = LANES // 64


def _wkv_kernel(kk_ref, kka_ref, d_ref, dr_ref, k_ref, v_ref, c1_ref, c2_ref, s0_ref,
                o_ref, sfin_ref, s_ref, mask_ref, *, chunk):
    f32, bf16 = jnp.float32, jnp.bfloat16
    hd = 64
    c = pl.program_id(1)
    n_col = s_ref.shape[2] // LANES

    @pl.when(c == 0)
    def _():
        s_ref[...] = s0_ref[0]
        lane = lax.broadcasted_iota(jnp.int32, mask_ref.shape, 2)
        row = lax.broadcasted_iota(jnp.int32, mask_ref.shape, 0)
        mask_ref[...] = ((lane % hd) == row).astype(f32)

    ones_bd = (lax.broadcasted_iota(jnp.int32, (LANES, LANES), 0) // hd
               == lax.broadcasted_iota(jnp.int32, (LANES, LANES), 1) // hd).astype(bf16)
    rows = hd * WKV_GROUP

    def seg(x3):
        x = x3.reshape(rows, LANES)
        hi = x.astype(bf16)
        lo = (x - hi.astype(f32)).astype(bf16)
        y = (jnp.dot(hi, ones_bd, preferred_element_type=f32)
             + jnp.dot(lo, ones_bd, preferred_element_type=f32))
        return y.reshape(hd, WKV_GROUP, LANES)

    def step(t, carry):
        kk, kka, d, dr = kk_ref[0, t], kka_ref[0, t], d_ref[0, t], dr_ref[0, t]
        k, v, c1, c2 = k_ref[0, t], v_ref[0, t], c1_ref[0, t], c2_ref[0, t]
        v_hi = v.astype(bf16).astype(f32)
        v_lo = v - v_hi
        for p in range(n_col):
            sl = slice(p * LANES, (p + 1) * LANES)
            m = mask_ref[...]
            s = s_ref[:, :, sl]
            sa = seg(s * kk[None, :, sl])
            tb = seg(s * dr[None, :, sl])
            vb = (jnp.dot((m * v_hi[None, :, sl]).reshape(rows, LANES).astype(bf16), ones_bd, preferred_element_type=f32)
                  + jnp.dot((m * v_lo[None, :, sl]).reshape(rows, LANES).astype(bf16), ones_bd, preferred_element_type=f32)
                  ).reshape(hd, WKV_GROUP, LANES)
            s_ref[:, :, sl] = s * d[None, :, sl] - sa * kka[None, :, sl] + vb * k[None, :, sl]
            ob = tb - sa * c1[None, :, sl] + vb * c2[None, :, sl]
            o_ref[0, t, :, sl] = jnp.sum(ob * m, axis=0)
        return carry

    lax.fori_loop(0, chunk, step, 0)

    @pl.when(c == pl.num_programs(1) - 1)
    def _():
        sfin_ref[0] = s_ref[...]


def wkv_scan(r_h, k_h, v_h, decay, kk, a_h, wkv0):
    B, T, H, N = r_h.shape
    width = H * N
    G = B // WKV_GROUP
    chunk = min(WKV_CHUNK, T)
    kka = kk * a_h
    dr = decay * r_h
    c1 = jnp.broadcast_to(jnp.sum(kka * r_h, -1, keepdims=True), r_h.shape)
    c2 = jnp.broadcast_to(jnp.sum(k_h * r_h, -1, keepdims=True), r_h.shape)
    tm = lambda t: t.reshape(G, WKV_GROUP, T, width).transpose(0, 2, 1, 3)
    s0 = wkv0.reshape(G, WKV_GROUP, H, N, N).transpose(0, 3, 1, 2, 4).reshape(G, N, WKV_GROUP, width)
    seq_spec = pl.BlockSpec((1, chunk, WKV_GROUP, width), lambda g, c: (g, c, 0, 0))
    st_spec = pl.BlockSpec((1, N, WKV_GROUP, width), lambda g, c: (g, 0, 0, 0))
    out, s_fin = pl.pallas_call(
        functools.partial(_wkv_kernel, chunk=chunk),
        grid=(G, T // chunk),
        in_specs=[seq_spec] * 8 + [st_spec],
        out_specs=[seq_spec, st_spec],
        out_shape=[jax.ShapeDtypeStruct((G, T, WKV_GROUP, width), jnp.float32),
                   jax.ShapeDtypeStruct((G, N, WKV_GROUP, width), jnp.float32)],
        scratch_shapes=[pltpu.VMEM((N, WKV_GROUP, width), jnp.float32),
                        pltpu.VMEM((N, WKV_GROUP, LANES), jnp.float32)],
        compiler_params=pltpu.CompilerParams(dimension_semantics=("arbitrary", "arbitrary")),
        name="wkv_scan",
    )(tm(kk), tm(kka), tm(decay), tm(dr), tm(k_h), tm(v_h), tm(c1), tm(c2), s0)
    out = out.transpose(0, 2, 1, 3).reshape(B, T, H, N)
    s_fin = s_fin.reshape(G, N, WKV_GROUP, H, N).transpose(0, 2, 3, 1, 4).reshape(B, H, N, N)
    return out, s_fin


def rwkv_mix(zr, z_prev0, wkv0, lp):
    B, T, _ = zr.shape
    f32 = jnp.float32
    z_prev = jnp.concatenate([z_prev0[:, None, :].astype(zr.dtype), zr[:, :-1]], axis=1)
    xs = zr + (z_prev - zr) * lp['rwkv_mu']
    r, k, v, wl, al, gl = jnp.split(xs, RWKV_SPLITS, axis=-1)
    w = -jax.nn.softplus(-(lp['rwkv_w0'] + jnp.tanh(wl) @ lp['rwkv_w_up'])) - 0.5
    a = jax.nn.sigmoid(lp['rwkv_a0'] + al @ lp['rwkv_a_up'])
    g = jax.nn.sigmoid(gl) @ lp['rwkv_g_up']
    heads = lambda t: t.astype(f32).reshape(B, T, RWKV_HEADS, HEAD_DIM)
    kk = heads(k * lp['rwkv_k_k'])
    kk = kk / jnp.maximum(jnp.sqrt(jnp.sum(kk * kk, -1, keepdims=True)), 1e-12)
    k = k * (1.0 + (a - 1.0) * lp['rwkv_k_a'])
    r_h, k_h, v_h, a_h = heads(r), heads(k), heads(v), heads(a)
    decay = jnp.exp(-jnp.exp(heads(w)))

    out, s_fin = wkv_scan(r_h, k_h, v_h, decay, kk, a_h, wkv0.astype(f32))
    mu = jnp.mean(out, -1, keepdims=True)
    var = jnp.mean(jnp.square(out - mu), -1, keepdims=True)
    on = ((out - mu) * lax.rsqrt(var + GN_EPS)).reshape(B, T, RWKV_WIDTH) * lp['rwkv_gn_w'] + lp['rwkv_gn_b']
    bonus = jnp.sum(r_h * k_h * lp['rwkv_r_k'], -1, keepdims=True) * v_h
    o = (on + bonus.reshape(B, T, RWKV_WIDTH)) * g
    return o.astype(zr.dtype), s_fin, zr[:, -1]


def nsa_project(zn, pos, lp):
    B, T, _ = zn.shape
    q = zn[..., :NSA_WIDTH].reshape(B, T, NSA_HEADS, HEAD_DIM)
    kv = zn[..., NSA_WIDTH:NSA_WIDTH + 6 * KV_WIDTH].reshape(B, T, N_NSA_BRANCH, 2, NSA_KV, HEAD_DIM)
    gates = jax.nn.sigmoid(zn[..., NSA_WIDTH + 6 * KV_WIDTH:].reshape(B, T, NSA_HEADS, N_NSA_BRANCH))
    qn = rmsnorm(q, lp['q_norm'])
    qr = rope(qn, pos)

    def prep(kvb, gk):
        return jnp.stack([rope(rmsnorm(kvb[:, :, 0], gk), pos), kvb[:, :, 1]], axis=2)

    return qn, qr, kv[:, :, 0], prep(kv[:, :, 1], lp['k_norm'][1]), prep(kv[:, :, 2], lp['k_norm'][2]), gates


def compress(kv_raw, lp):
    n_chunk = kv_raw.shape[0] // CMP_STRIDE
    nc = n_chunk - CMP_RATIO + 1
    chunks = kv_raw[:n_chunk * CMP_STRIDE].reshape(n_chunk, CMP_STRIDE, 2, NSA_KV, HEAD_DIM)
    w1 = lp['cmp_w1'].reshape(2, CMP_RATIO, CMP_STRIDE, HEAD_DIM, CMP_HIDDEN)
    part = jnp.einsum('csekd,ersdf->crekf', chunks, w1)
    hid = part[0:nc, 0]
    for r in range(1, CMP_RATIO):
        hid = hid + part[r:r + nc, r]
    pe_bias = jnp.einsum('esd,esdf->ef', lp['cmp_pe'], lp['cmp_w1'].reshape(2, CMP_BLOCK, HEAD_DIM, CMP_HIDDEN))
    hid = jax.nn.gelu(hid + pe_bias[None, :, None, :])
    out = jnp.einsum('nekf,efd->nekd', hid, lp['cmp_w2'])
    kc = rmsnorm(out[:, 0], lp['k_norm'][0])
    return kc, out[:, 1], jnp.arange(nc) * CMP_STRIDE + CMP_BLOCK - 1


def block_cover_matrix(nsb, nc):
    c0 = jnp.arange(nc) * CMP_STRIDE
    s0 = jnp.arange(nsb) * SEL_BLOCK
    cover = (c0[None, :] <= s0[:, None] + SEL_BLOCK - 1) & (c0[None, :] + CMP_BLOCK - 1 >= s0[:, None])
    return cover.astype(jnp.float32)


def nsa_core(qn, qr, q_pos, kc, vc, c_end, nsb, npb, gather_past, kv_new, kv_win, win_pos, gates):
    tq = qn.shape[0]
    scale = HEAD_DIM ** -0.5
    qn_g = qn.reshape(tq, NSA_KV, NSA_GROUP, HEAD_DIM)
    qr_g = qr.reshape(tq, NSA_KV, NSA_GROUP, HEAD_DIM)
    causal = q_pos[:, None] >= q_pos[None, :]
    p_c = masked_softmax(jnp.einsum('tkgd,nkd->kgtn', qn_g, kc) * scale, c_end[None, :] <= q_pos[:, None])
    o_c = jnp.einsum('kgtn,nkd->tkgd', p_c, vc)
    imp = jnp.einsum('kgtn,jn->ktj', p_c, block_cover_matrix(nsb, kc.shape[0]))
    blk = jnp.arange(nsb)[None, :]
    cur = (q_pos // SEL_BLOCK)[:, None]
    forced = (blk == cur) | (blk == cur - 1) | (blk == 0)
    imp = jnp.where(forced, jnp.inf, jnp.where(blk > cur, -jnp.inf, imp))
    n_sel = min(N_SELECT, nsb)
    sel = lax.top_k(imp, n_sel)[1]
    kv_p = gather_past(sel)
    m_len = n_sel * SEL_BLOCK
    k_p = kv_p[..., 0, :].reshape(NSA_KV, tq, m_len, HEAD_DIM)
    v_p = kv_p[..., 1, :].reshape(NSA_KV, tq, m_len, HEAD_DIM)
    s_sel = jnp.concatenate([jnp.einsum('tkgd,ktmd->kgtm', qr_g, k_p),
                             jnp.einsum('tkgd,skd->kgts', qr_g, kv_new[:, 0])], -1) * scale
    m_past = jnp.repeat(sel < npb, SEL_BLOCK, axis=-1)
    member = jnp.any(sel[..., None] == (q_pos // SEL_BLOCK), axis=-2)
    p_s = masked_softmax(s_sel, jnp.concatenate([m_past, member & causal], -1)[:, None])
    o_s = (jnp.einsum('kgtm,ktmd->tkgd', p_s[..., :m_len], v_p)
           + jnp.einsum('kgts,skd->tkgd', p_s[..., m_len:], kv_new[:, 1]))
    dist = q_pos[:, None] - win_pos[None, :]
    m_w = (dist >= 0) & (dist < WINDOW) & (win_pos >= 0)[None, :]
    p_w = masked_softmax(jnp.einsum('tkgd,skd->kgts', qr_g, kv_win[:, 0]) * scale, m_w)
    o_w = jnp.einsum('kgts,skd->tkgd', p_w, kv_win[:, 1])
    g = gates.reshape(tq, NSA_KV, NSA_GROUP, N_NSA_BRANCH)
    o = o_c * g[..., 0:1] + o_s * g[..., 1:2] + o_w * g[..., 2:3]
    return o.reshape(tq, NSA_WIDTH).astype(qn.dtype)


SEL_COLS = 64
SEL_KT = 256
CHUNK_W = CMP_STRIDE * HEAD_DIM
NEG_BIG = -1e30


def _dot_nt(a, b):
    return lax.dot_general(a, b, (((1,), (1,)), ((), ())), preferred_element_type=jnp.float32)


def _compress_kernel(c_ref, cn_ref, w1_ref, pe_ref, w2_ref, kn_ref, kc_ref, vc_ref):
    f32, bf16 = jnp.float32, jnp.bfloat16
    for e in range(2):
        w1 = w1_ref[e].astype(bf16)
        hid = (jnp.dot(c_ref[0, e, 0].astype(bf16), w1[:CHUNK_W], preferred_element_type=f32)
               + jnp.dot(cn_ref[0, e, 0].astype(bf16), w1[CHUNK_W:], preferred_element_type=f32))
        pe_bias = jnp.dot(pe_ref[e].astype(bf16), w1, preferred_element_type=f32)[0:1]
        hid = jax.nn.gelu(hid + pe_bias)
        out = jnp.dot(hid.astype(bf16), w2_ref[e].astype(bf16), preferred_element_type=f32)
        if e == 0:
            out = out * lax.rsqrt(jnp.mean(out * out, -1, keepdims=True) + RMS_EPS) * kn_ref[...]
            kc_ref[0, 0] = out
        else:
            vc_ref[0, 0] = out


def compress_prompt(kv_cmp, lp):
    B, S = kv_cmp.shape[:2]
    nch = S // CMP_STRIDE
    c = kv_cmp.transpose(0, 2, 3, 1, 4).reshape(B, 2, NSA_KV, nch, CHUNK_W)
    cn = jnp.concatenate([c[:, :, :, 1:], jnp.zeros_like(c[:, :, :, :1])], axis=3)
    pe = jnp.broadcast_to(lp['cmp_pe'].reshape(2, 1, CMP_BLOCK * HEAD_DIM), (2, SUBLANES, CMP_BLOCK * HEAD_DIM))
    c_spec = pl.BlockSpec((1, 2, 1, nch, CHUNK_W), lambda b, k: (b, 0, k, 0, 0))
    full = lambda shape: pl.BlockSpec(shape, lambda b, k: (0,) * len(shape))
    o_spec = pl.BlockSpec((1, 1, nch, HEAD_DIM), lambda b, k: (b, k, 0, 0))
    return pl.pallas_call(
        _compress_kernel,
        grid=(B, NSA_KV),
        in_specs=[c_spec, c_spec, full((2, CMP_BLOCK * HEAD_DIM, CMP_HIDDEN)), full((2, SUBLANES, CMP_BLOCK * HEAD_DIM)),
                  full((2, CMP_HIDDEN, HEAD_DIM)), full((1, HEAD_DIM))],
        out_specs=[o_spec, o_spec],
        out_shape=[jax.ShapeDtypeStruct((B, NSA_KV, nch, HEAD_DIM), jnp.float32)] * 2,
        compiler_params=pltpu.CompilerParams(dimension_semantics=("arbitrary", "arbitrary")),
        name="nsa_compress",
    )(c, cn, lp['cmp_w1'], pe, lp['cmp_w2'], lp['k_norm'][0:1])


def _flash_update(q, kt, vt, valid, m, l, acc, scale):
    s = jnp.where(valid, _dot_nt(q, kt) * scale, NEG_BIG)
    m_new = jnp.maximum(m, jnp.max(s, -1, keepdims=True))
    p = jnp.where(valid, jnp.exp(s - m_new), 0.0)
    alpha = jnp.exp(m - m_new)
    l = alpha * l + jnp.sum(p, -1, keepdims=True)
    acc = alpha * acc + jnp.dot(p.astype(jnp.bfloat16), vt, preferred_element_type=jnp.float32)
    return m_new, l, acc


def _nsa_prompt_kernel(qn_ref, qr_ref, kc_ref, vc_ref, ks_ref, vs_ref, kw_ref, vw_ref, g_ref, o_ref):
    f32, bf16 = jnp.float32, jnp.bfloat16
    i = pl.program_id(2)
    t0 = i * Q_BLOCK
    scale = HEAD_DIM ** -0.5
    nch = kc_ref.shape[2]
    t_col = t0 + lax.broadcasted_iota(jnp.int32, (Q_BLOCK, 1), 0)

    kc = kc_ref[0, 0].astype(bf16)
    vc = vc_ref[0, 0].astype(bf16)
    n_row = lax.broadcasted_iota(jnp.int32, (1, nch), 1)
    c_valid = (n_row * CMP_STRIDE + (CMP_BLOCK - 1) <= t_col) & (n_row < nch - 1)
    p_sum = jnp.zeros((Q_BLOCK, nch), f32)
    o_cmp = []
    for g in range(NSA_GROUP):
        s = jnp.where(c_valid, _dot_nt(qn_ref[0, 0, g].astype(bf16), kc) * scale, -jnp.inf)
        m = jnp.max(s, -1, keepdims=True)
        m = jnp.where(m > -jnp.inf, m, 0.0)
        e = jnp.exp(s - m)
        p = e / jnp.maximum(jnp.sum(e, -1, keepdims=True), 1e-30)
        p_sum = p_sum + p
        o_cmp.append(jnp.dot(p.astype(bf16), vc, preferred_element_type=f32))

    n_col = lax.broadcasted_iota(jnp.int32, (nch, SEL_COLS), 0) * CMP_STRIDE
    j_blk = lax.broadcasted_iota(jnp.int32, (nch, SEL_COLS), 1) * SEL_BLOCK
    cover = ((n_col <= j_blk + (SEL_BLOCK - 1)) & (n_col + (CMP_BLOCK - 1) >= j_blk)).astype(bf16)
    p_hi = p_sum.astype(bf16)
    p_lo = (p_sum - p_hi.astype(f32)).astype(bf16)
    imp = jnp.dot(p_hi, cover, preferred_element_type=f32) + jnp.dot(p_lo, cover, preferred_element_type=f32)
    j_row = lax.broadcasted_iota(jnp.int32, (1, SEL_COLS), 1)
    cur = t_col // SEL_BLOCK
    forced = (j_row == cur) | (j_row == cur - 1) | (j_row == 0)
    imp = jnp.where(forced, jnp.inf, jnp.where(j_row > cur, -jnp.inf, imp))
    rank = jnp.zeros((Q_BLOCK, SEL_COLS), f32)
    for jp in range(SEL_COLS):
        col = imp[:, jp:jp + 1]
        rank = rank + jnp.where((col > imp) | ((col == imp) & (jp < j_row)), 1.0, 0.0)
    sel_mask = jnp.where(rank < N_SELECT, 1.0, 0.0).astype(bf16)

    q_rot = [qr_ref[0, 0, g].astype(bf16) for g in range(NSA_GROUP)]

    def init():
        return tuple((jnp.full((Q_BLOCK, 1), NEG_BIG, f32), jnp.zeros((Q_BLOCK, 1), f32),
                      jnp.zeros((Q_BLOCK, HEAD_DIM), f32)) for _ in range(NSA_GROUP))

    def sel_body(c, carry):
        k0 = pl.multiple_of(c * SEL_KT, SEL_KT)
        kt = ks_ref[0, 0, pl.ds(k0, SEL_KT), :].astype(bf16)
        vt = vs_ref[0, 0, pl.ds(k0, SEL_KT), :].astype(bf16)
        kpos = k0 + lax.broadcasted_iota(jnp.int32, (1, SEL_KT), 1)
        expand = (lax.broadcasted_iota(jnp.int32, (SEL_COLS, SEL_KT), 0)
                  == (k0 + lax.broadcasted_iota(jnp.int32, (SEL_COLS, SEL_KT), 1)) // SEL_BLOCK).astype(bf16)
        valid = (jnp.dot(sel_mask, expand, preferred_element_type=f32) > 0.5) & (kpos <= t_col)
        return tuple(_flash_update(q_rot[g], kt, vt, valid, *carry[g], scale) for g in range(NSA_GROUP))

    sel = lax.fori_loop(0, i // (SEL_KT // Q_BLOCK) + 1, sel_body, init())

    def win_body(c, carry):
        k0 = pl.multiple_of(c * Q_BLOCK, Q_BLOCK)
        kt = kw_ref[0, 0, pl.ds(k0, Q_BLOCK), :].astype(bf16)
        vt = vw_ref[0, 0, pl.ds(k0, Q_BLOCK), :].astype(bf16)
        dist = t_col - (k0 + lax.broadcasted_iota(jnp.int32, (1, Q_BLOCK), 1))
        valid = (dist >= 0) & (dist < WINDOW)
        return tuple(_flash_update(q_rot[g], kt, vt, valid, *carry[g], scale) for g in range(NSA_GROUP))

    win = lax.fori_loop(jnp.maximum(i - WINDOW // Q_BLOCK, 0), i + 1, win_body, init())

    for g in range(NSA_GROUP):
        gate = g_ref[0, 0, g]
        o_sel = sel[g][2] / jnp.maximum(sel[g][1], 1e-30)
        o_win = win[g][2] / jnp.maximum(win[g][1], 1e-30)
        o_ref[0, 0, g] = o_cmp[g] * gate[:, 0:1] + o_sel * gate[:, 1:2] + o_win * gate[:, 2:3]


def nsa_prompt(qn, qr, kv_cmp, kv_sel, kv_win, gates, lp):
    B, S = qn.shape[:2]
    assert S % SEL_KT == 0 and S <= SEL_COLS * SEL_BLOCK
    nch = S // CMP_STRIDE
    kc, vc = compress_prompt(kv_cmp, lp)
    heads = lambda t: t.reshape(B, S, NSA_KV, NSA_GROUP, t.shape[-1]).transpose(0, 2, 3, 1, 4)
    kv_major = lambda t: t.transpose(0, 2, 1, 3)
    q_spec = pl.BlockSpec((1, 1, NSA_GROUP, Q_BLOCK, HEAD_DIM), lambda b, k, i: (b, k, 0, i, 0))
    g_spec = pl.BlockSpec((1, 1, NSA_GROUP, Q_BLOCK, N_NSA_BRANCH), lambda b, k, i: (b, k, 0, i, 0))
    c_spec = pl.BlockSpec((1, 1, nch, HEAD_DIM), lambda b, k, i: (b, k, 0, 0))
    s_spec = pl.BlockSpec((1, 1, S, HEAD_DIM), lambda b, k, i: (b, k, 0, 0))
    o = pl.pallas_call(
        _nsa_prompt_kernel,
        grid=(B, NSA_KV, S // Q_BLOCK),
        in_specs=[q_spec, q_spec, c_spec, c_spec, s_spec, s_spec, s_spec, s_spec, g_spec],
        out_specs=q_spec,
        out_shape=jax.ShapeDtypeStruct((B, NSA_KV, NSA_GROUP, S, HEAD_DIM), jnp.float32),
        compiler_params=pltpu.CompilerParams(dimension_semantics=("arbitrary", "arbitrary", "arbitrary")),
        name="nsa_prompt",
    )(heads(qn), heads(qr), kc, vc, kv_major(kv_sel[:, :, 0]), kv_major(kv_sel[:, :, 1]),
      kv_major(kv_win[:, :, 0]), kv_major(kv_win[:, :, 1]), heads(gates))
    return o.transpose(0, 3, 1, 2, 4).reshape(B, S, NSA_WIDTH)


PAGE_GROUP = 8
PAGE_CHUNKS = PAGE_SIZE // CMP_STRIDE
KV_ROW = 2 * KV_WIDTH
SAMPLE_ROWS = NSA_HEADS * DEC_SEQ
SEL_COLS_S = 256


def _page_specs(block, n_pages):
    zeros = (0,) * (len(block) - 1)
    return [pl.BlockSpec(block, (lambda b, c, pt, r=r: (pt[b * n_pages + c * PAGE_GROUP + r],) + zeros))
            for r in range(PAGE_GROUP)]


def _compress_sample_kernel(pt_ref, *refs):
    f32, bf16 = jnp.float32, jnp.bfloat16
    pages = refs[:PAGE_GROUP]
    w1_ref, pe_ref, w2_ref, kn_ref, kc_ref, vc_ref, p0_sc, p1_sc = refs[PAGE_GROUP:]
    c = pl.program_id(1)
    rows = PAGE_GROUP * PAGE_CHUNKS
    r0 = pl.multiple_of(c * rows, rows)
    for e in range(2):
        w1 = w1_ref[e].astype(bf16)
        for k in range(NSA_KV):
            cm = jnp.concatenate([pg[0, e, k] for pg in pages], axis=0).astype(bf16)
            p0_sc[e, k, pl.ds(r0, rows), :] = jnp.dot(cm, w1[:CHUNK_W], preferred_element_type=f32)
            p1_sc[e, k, pl.ds(r0, rows), :] = jnp.dot(cm, w1[CHUNK_W:], preferred_element_type=f32)

    @pl.when(c == pl.num_programs(1) - 1)
    def _():
        nch = p0_sc.shape[2]
        for e in range(2):
            pe_bias = jnp.dot(pe_ref[e].astype(bf16), w1_ref[e].astype(bf16), preferred_element_type=f32)[0:1]
            out = jnp.zeros((nch, LANES), f32)
            for k in range(NSA_KV):
                nxt = pltpu.roll(p1_sc[e, k], nch - 1, 0)
                hid = jax.nn.gelu(p0_sc[e, k] + nxt + pe_bias)
                out = out + jnp.dot(hid.astype(bf16), w2_ref[e, k].astype(bf16), preferred_element_type=f32)
            if e == 0:
                lane = lax.broadcasted_iota(jnp.int32, (1, LANES), 1)
                sq = out * out
                ms0 = jnp.sum(jnp.where(lane < HEAD_DIM, sq, 0.0), -1, keepdims=True) / HEAD_DIM
                ms1 = jnp.sum(jnp.where(lane >= HEAD_DIM, sq, 0.0), -1, keepdims=True) / HEAD_DIM
                kc_ref[0] = out * jnp.where(lane < HEAD_DIM, lax.rsqrt(ms0 + RMS_EPS), lax.rsqrt(ms1 + RMS_EPS)) * kn_ref[...]
            else:
                vc_ref[0] = out


def _nsa_sample_kernel(pt_ref, *refs):
    f32, bf16 = jnp.float32, jnp.bfloat16
    pages = refs[:PAGE_GROUP]
    (qn_ref, qr_ref, kc_ref, vc_ref, knew_ref, win_ref, g_ref, o_ref,
     m_sc, l_sc, acc_sc, selm_sc, ocmp_sc) = refs[PAGE_GROUP:]
    c = pl.program_id(1)
    scale = HEAD_DIM ** -0.5
    T = DEC_SEQ
    nch = kc_ref.shape[1]
    keys = PAGE_GROUP * PAGE_SIZE
    t_row = lax.broadcasted_iota(jnp.int32, (SAMPLE_ROWS, 1), 0) % T
    q_rot = qr_ref[0].astype(bf16)

    @pl.when(c == 0)
    def _():
        n_row = lax.broadcasted_iota(jnp.int32, (1, nch), 1)
        s = jnp.where(n_row < nch - 1, _dot_nt(qn_ref[0].astype(bf16), kc_ref[0].astype(bf16)) * scale, -jnp.inf)
        e = jnp.exp(s - jnp.max(s, -1, keepdims=True))
        p = e / jnp.maximum(jnp.sum(e, -1, keepdims=True), 1e-30)
        ocmp_sc[...] = jnp.dot(p.astype(bf16), vc_ref[0].astype(bf16), preferred_element_type=f32)
        grp = NSA_GROUP * T
        p_sum = jnp.concatenate(
            [sum(p[k * grp + g * T:k * grp + (g + 1) * T] for g in range(NSA_GROUP)) for k in range(NSA_KV)], axis=0)
        n_col = lax.broadcasted_iota(jnp.int32, (nch, SEL_COLS_S), 0) * CMP_STRIDE
        j_blk = lax.broadcasted_iota(jnp.int32, (nch, SEL_COLS_S), 1) * SEL_BLOCK
        cover = ((n_col <= j_blk + (SEL_BLOCK - 1)) & (n_col + (CMP_BLOCK - 1) >= j_blk)).astype(bf16)
        p_hi = p_sum.astype(bf16)
        p_lo = (p_sum - p_hi.astype(f32)).astype(bf16)
        imp = jnp.dot(p_hi, cover, preferred_element_type=f32) + jnp.dot(p_lo, cover, preferred_element_type=f32)
        j_row = lax.broadcasted_iota(jnp.int32, (1, SEL_COLS_S), 1)
        cur = (PAST_LEN + lax.broadcasted_iota(jnp.int32, (NSA_KV * T, 1), 0) % T) // SEL_BLOCK
        forced = (j_row == cur) | (j_row == cur - 1) | (j_row == 0)
        imp = jnp.where(forced, jnp.inf, jnp.where(j_row > cur, -jnp.inf, imp))
        rank = jnp.zeros((NSA_KV * T, SEL_COLS_S), f32)
        for jp in range(-(-(PAST_LEN + T) // SEL_BLOCK)):
            col = imp[:, jp:jp + 1]
            rank = rank + jnp.where((col > imp) | ((col == imp) & (jp < j_row)), 1.0, 0.0)
        sel16 = jnp.where(rank < N_SELECT, 1.0, 0.0)
        selm_sc[...] = jnp.concatenate([sel16[k * T:(k + 1) * T] for k in range(NSA_KV) for _ in range(NSA_GROUP)], axis=0)
        m_sc[...] = jnp.full(m_sc.shape, NEG_BIG, f32)
        l_sc[...] = jnp.zeros(l_sc.shape, f32)
        acc_sc[...] = jnp.zeros(acc_sc.shape, f32)

    def update(kt, vt, valid):
        m, l, acc = _flash_update(q_rot, kt, vt, valid, m_sc[:, 0:1], l_sc[:, 0:1], acc_sc[...], scale)
        m_sc[...] = jnp.broadcast_to(m, m_sc.shape)
        l_sc[...] = jnp.broadcast_to(l, l_sc.shape)
        acc_sc[...] = acc

    kt = jnp.concatenate([pg[0, :, 0:KV_WIDTH] for pg in pages], axis=0).astype(bf16)
    vt = jnp.concatenate([pg[0, :, KV_WIDTH:KV_ROW] for pg in pages], axis=0).astype(bf16)
    past_cols = PAST_LEN // SEL_BLOCK
    expand = (lax.broadcasted_iota(jnp.int32, (past_cols, keys), 0)
              == c * (keys // SEL_BLOCK) + lax.broadcasted_iota(jnp.int32, (past_cols, keys), 1) // SEL_BLOCK).astype(bf16)
    update(kt, vt, jnp.dot(selm_sc[:, 0:past_cols].astype(bf16), expand, preferred_element_type=f32) > 0.5)

    @pl.when(c == pl.num_programs(1) - 1)
    def _():
        s_idx = lax.broadcasted_iota(jnp.int32, (1, LANES), 1)
        member = selm_sc[:, past_cols:past_cols + 1] > 0.5
        update(knew_ref[0, :, 0:KV_WIDTH].astype(bf16), knew_ref[0, :, KV_WIDTH:KV_ROW].astype(bf16),
               member & (s_idx <= t_row) & (s_idx < T))
        o_sel = acc_sc[...] / jnp.maximum(l_sc[:, 0:1], 1e-30)
        n_win = win_ref.shape[1]
        win_cache = min(WINDOW, PAST_LEN)
        w_idx = lax.broadcasted_iota(jnp.int32, (1, n_win), 1)
        dist = (PAST_LEN + t_row) - (PAST_LEN - win_cache + w_idx)
        valid = (dist >= 0) & (dist < WINDOW) & (w_idx < win_cache + T)
        m, l, acc = _flash_update(q_rot, win_ref[0, :, 0:KV_WIDTH].astype(bf16), win_ref[0, :, KV_WIDTH:KV_ROW].astype(bf16),
                                  valid, jnp.full((SAMPLE_ROWS, 1), NEG_BIG, f32), jnp.zeros((SAMPLE_ROWS, 1), f32),
                                  jnp.zeros((SAMPLE_ROWS, LANES), f32), scale)
        o_win = acc / jnp.maximum(l, 1e-30)
        gate = g_ref[0]
        o_ref[0] = ocmp_sc[...] * gate[:, 0:1] + o_sel * gate[:, 1:2] + o_win * gate[:, 2:3]


def nsa_sample(qn, qr, kv_cmp, kv_sel, kv_win, gates, cache_cmp, cache_sel, page_table, state_win, lp):
    B, T = qn.shape[:2]
    n_pages = PAST_LEN // PAGE_SIZE
    n_phys = cache_cmp.shape[0]
    nch = PAST_LEN // CMP_STRIDE
    assert T == DEC_SEQ == SUBLANES and (PAST_LEN + T) // CMP_STRIDE == nch and n_pages % PAGE_GROUP == 0
    assert -(-(PAST_LEN + T) // SEL_BLOCK) <= SEL_COLS_S
    pt = page_table.reshape(B * n_pages)
    n_steps = n_pages // PAGE_GROUP
    const = lambda shape: pl.BlockSpec(shape, lambda b, c, pt: (0,) * len(shape))
    per_seq = lambda shape: pl.BlockSpec((1,) + shape, lambda b, c, pt: (b,) + (0,) * len(shape))
    params = pltpu.CompilerParams(dimension_semantics=("arbitrary", "arbitrary"))

    cmp_pages = cache_cmp.reshape(n_phys, PAGE_CHUNKS, CMP_STRIDE, 2, NSA_KV, HEAD_DIM).transpose(0, 3, 4, 1, 2, 5)
    cmp_pages = cmp_pages.reshape(n_phys, 2, NSA_KV, PAGE_CHUNKS, CHUNK_W)
    pe = jnp.broadcast_to(lp['cmp_pe'].reshape(2, 1, CMP_BLOCK * HEAD_DIM), (2, SUBLANES, CMP_BLOCK * HEAD_DIM))
    w2 = lp['cmp_w2']
    zero = jnp.zeros_like(w2)
    w2_half = jnp.stack([jnp.concatenate([w2, zero], -1), jnp.concatenate([zero, w2], -1)], axis=1)
    kn = jnp.tile(lp['k_norm'][0:1], (1, NSA_KV))
    kc, vc = pl.pallas_call(
        _compress_sample_kernel,
        grid_spec=pltpu.PrefetchScalarGridSpec(
            num_scalar_prefetch=1, grid=(B, n_steps),
            in_specs=_page_specs((1, 2, NSA_KV, PAGE_CHUNKS, CHUNK_W), n_pages)
            + [const((2, CMP_BLOCK * HEAD_DIM, CMP_HIDDEN)), const((2, SUBLANES, CMP_BLOCK * HEAD_DIM)),
               const((2, NSA_KV, CMP_HIDDEN, LANES)), const((1, LANES))],
            out_specs=[per_seq((nch, LANES)), per_seq((nch, LANES))],
            scratch_shapes=[pltpu.VMEM((2, NSA_KV, nch, CMP_HIDDEN), jnp.float32)] * 2),
        out_shape=[jax.ShapeDtypeStruct((B, nch, LANES), jnp.float32)] * 2,
        compiler_params=params,
        name="nsa_compress_sample",
    )(pt, *([cmp_pages] * PAGE_GROUP), lp['cmp_w1'], pe, w2_half, kn)

    def rows(t):
        return t.reshape(B, T, NSA_KV, NSA_GROUP, t.shape[-1]).transpose(0, 2, 3, 1, 4).reshape(B, SAMPLE_ROWS, t.shape[-1])

    def lane_halves(q):
        first = (jnp.arange(SAMPLE_ROWS) < SAMPLE_ROWS // NSA_KV)[None, :, None]
        return jnp.concatenate([jnp.where(first, q, 0.0), jnp.where(first, 0.0, q)], -1)

    flat = lambda t: t.reshape(t.shape[0], t.shape[1], KV_ROW)
    pad_rows = lambda t, n: jnp.pad(t, ((0, 0), (0, n - t.shape[1]), (0, 0)))
    k_new = pad_rows(flat(kv_sel), LANES)
    win_cache = state_win.shape[1]
    n_win = -(-(win_cache + T) // LANES) * LANES
    win_all = pad_rows(jnp.concatenate([flat(state_win), flat(kv_win)], 1), n_win)
    sel_pages = cache_sel.reshape(n_phys, PAGE_SIZE, KV_ROW)
    o = pl.pallas_call(
        _nsa_sample_kernel,
        grid_spec=pltpu.PrefetchScalarGridSpec(
            num_scalar_prefetch=1, grid=(B, n_steps),
            in_specs=_page_specs((1, PAGE_SIZE, KV_ROW), n_pages)
            + [per_seq((SAMPLE_ROWS, LANES)), per_seq((SAMPLE_ROWS, LANES)), per_seq((nch, LANES)), per_seq((nch, LANES)),
               per_seq((LANES, KV_ROW)), per_seq((n_win, KV_ROW)), per_seq((SAMPLE_ROWS, N_NSA_BRANCH))],
            out_specs=per_seq((SAMPLE_ROWS, LANES)),
            scratch_shapes=[pltpu.VMEM((SAMPLE_ROWS, LANES), jnp.float32)] * 3
            + [pltpu.VMEM((SAMPLE_ROWS, SEL_COLS_S), jnp.float32), pltpu.VMEM((SAMPLE_ROWS, LANES), jnp.float32)]),
        out_shape=jax.ShapeDtypeStruct((B, SAMPLE_ROWS, LANES), jnp.float32),
        compiler_params=params,
        name="nsa_sample",
    )(pt, *([sel_pages] * PAGE_GROUP), lane_halves(rows(qn)), lane_halves(rows(qr)), kc, vc, k_new, win_all, rows(gates))
    half = SAMPLE_ROWS // NSA_KV
    o = jnp.concatenate([o[:, :half, :HEAD_DIM], o[:, half:, HEAD_DIM:]], axis=1)
    return o.reshape(B, NSA_KV, NSA_GROUP, T, HEAD_DIM).transpose(0, 3, 1, 2, 4).reshape(B, T, NSA_WIDTH)


PEER_SLOTS = PEER_HEADS * PEER_TOPK
PEER_TOK = 128
HALF_D = D_MODEL // 2
ROW_SUB = HALF_D // LANES


def pack_expert_table(tab):
    bits = lax.bitcast_convert_type(tab.astype(jnp.bfloat16), jnp.uint16).astype(jnp.uint32)
    return ((bits[:, HALF_D:] << 16) | bits[:, :HALF_D]).reshape(tab.shape[0], ROW_SUB, LANES)


def _unpack_row(row):
    lo = lax.bitcast_convert_type(row << 16, jnp.float32)
    hi = lax.bitcast_convert_type(row & jnp.uint32(0xFFFF0000), jnp.float32)
    return lo, hi


def _load_table_and_indices(idx_hbm, tab_hbm, tab_vmem, idx_smem, sem):
    i = pl.program_id(0)

    @pl.when(i == 0)
    def _():
        cp = pltpu.make_async_copy(tab_hbm, tab_vmem, sem.at[0])
        cp.start()
        cp.wait()

    n = idx_smem.shape[0]
    cp = pltpu.make_async_copy(idx_hbm.at[pl.ds(i * n, n)], idx_smem, sem.at[1])
    cp.start()
    cp.wait()


def _eye():
    return (lax.broadcasted_iota(jnp.int32, (PEER_SLOTS, LANES), 0)
            == lax.broadcasted_iota(jnp.int32, (PEER_SLOTS, LANES), 1))


def _peer_act_kernel(idx_hbm, h_ref, gate_ref, tab_hbm, w_ref, tab_vmem, idx_smem, ps_ref, a_ref, sem):
    _load_table_and_indices(idx_hbm, tab_hbm, tab_vmem, idx_smem, sem)
    eye = _eye()

    def tok(t, carry):
        h_lo = h_ref[t, 0:ROW_SUB, :]
        h_hi = h_ref[t, ROW_SUB:2 * ROW_SUB, :]
        for j in range(PEER_SLOTS):
            lo, hi = _unpack_row(tab_vmem[idx_smem[t * PEER_SLOTS + j]])
            ps_ref[j * ROW_SUB:(j + 1) * ROW_SUB, :] = lo * h_lo + hi * h_hi
        q = ps_ref[pl.ds(0, PEER_SLOTS, stride=ROW_SUB), :]
        for s in range(1, ROW_SUB):
            q = q + ps_ref[pl.ds(s, PEER_SLOTS, stride=ROW_SUB), :]
        a_col = jnp.sum(q, axis=-1, keepdims=True)
        a_ref[pl.ds(t, 1), :] = jnp.sum(jnp.where(eye, a_col, 0.0), axis=0, keepdims=True)
        return carry

    lax.fori_loop(0, h_ref.shape[0], tok, 0)
    w_ref[...] = gate_ref[...] * jax.nn.gelu(a_ref[...])


def _peer_out_kernel(idx_hbm, w_ref, tab_hbm, o_ref, tab_vmem, idx_smem, wb_ref, sem):
    _load_table_and_indices(idx_hbm, tab_hbm, tab_vmem, idx_smem, sem)
    eye = _eye()
    n_acc = 4

    def tok(t, carry):
        w_row = w_ref[pl.ds(t, 1), :]
        w_col = jnp.sum(jnp.where(eye, w_row, 0.0), axis=-1, keepdims=True)
        wb_ref[...] = jnp.broadcast_to(w_col, (PEER_SLOTS, LANES))
        acc_lo = [jnp.zeros((ROW_SUB, LANES), jnp.float32) for _ in range(n_acc)]
        acc_hi = [jnp.zeros((ROW_SUB, LANES), jnp.float32) for _ in range(n_acc)]
        for j in range(PEER_SLOTS):
            lo, hi = _unpack_row(tab_vmem[idx_smem[t * PEER_SLOTS + j]])
            wj = wb_ref[j:j + 1, :]
            acc_lo[j % n_acc] = acc_lo[j % n_acc] + wj * lo
            acc_hi[j % n_acc] = acc_hi[j % n_acc] + wj * hi
        o_ref[t, 0:ROW_SUB, :] = (acc_lo[0] + acc_lo[1]) + (acc_lo[2] + acc_lo[3])
        o_ref[t, ROW_SUB:2 * ROW_SUB, :] = (acc_hi[0] + acc_hi[1]) + (acc_hi[2] + acc_hi[3])
        return carry

    lax.fori_loop(0, w_ref.shape[0], tok, 0)


def peer_experts(ht, e, gate, u_packed, v_packed):
    n = ht.shape[0]
    n_exp = u_packed.shape[0]
    tb = min(PEER_TOK, n)
    assert n % tb == 0 and tb % SUBLANES == 0
    idx =e.reshape(n * PEER_SLOTS)
    h3 = ht.reshape(n, 2 * ROW_SUB, LANES)
    any_spec = pl.BlockSpec(memory_space=pl.ANY)
    tok_spec = pl.BlockSpec((tb, PEER_SLOTS), lambda i: (i, 0))
    row_spec = pl.BlockSpec((tb, 2 * ROW_SUB, LANES), lambda i: (i, 0, 0))
    params = pltpu.CompilerParams(dimension_semantics=("arbitrary",), vmem_limit_bytes=VMEM_TABLE_LIMIT)
    table = pltpu.VMEM((n_exp, ROW_SUB, LANES), jnp.uint32)
    idx_smem = pltpu.SMEM((tb * PEER_SLOTS,), jnp.int32)
    w = pl.pallas_call(
        _peer_act_kernel,
        grid=(n // tb,),
        in_specs=[any_spec, row_spec, tok_spec, any_spec],
        out_specs=tok_spec,
        out_shape=jax.ShapeDtypeStruct((n, PEER_SLOTS), jnp.float32),
        scratch_shapes=[table, idx_smem, pltpu.VMEM((PEER_SLOTS * ROW_SUB, LANES), jnp.float32),
                        pltpu.VMEM((tb, PEER_SLOTS), jnp.float32), pltpu.SemaphoreType.DMA((2,))],
        compiler_params=params,
        name="peer_act",
    )(idx, h3, gate, u_packed)
    out = pl.pallas_call(
        _peer_out_kernel,
        grid=(n // tb,),
        in_specs=[any_spec, tok_spec, any_spec],
        out_specs=row_spec,
        out_shape=jax.ShapeDtypeStruct((n, 2 * ROW_SUB, LANES), jnp.float32),
        scratch_shapes=[table, idx_smem, pltpu.VMEM((PEER_SLOTS, LANES), jnp.float32),
                        pltpu.SemaphoreType.DMA((2,))],
        compiler_params=params,
        name="peer_out",
    )(idx, w, v_packed)
    return out.reshape(n, D_MODEL)


def peer_ffn(h, lp):
    lead = h.shape[:-1]
    ht = h.reshape(-1, D_MODEL)
    n = ht.shape[0]
    q = (ht @ lp['peer_wq']).reshape(n, PEER_HEADS, 2, PEER_DK // 2)
    s = jnp.einsum('thcd,hcnd->thcn', q, lp['peer_keys']).astype(jnp.float32)
    sv, si = lax.top_k(s, PEER_TOPK)
    cand = (sv[:, :, 0, :, None] + sv[:, :, 1, None, :]).reshape(n, PEER_HEADS, PEER_TOPK * PEER_TOPK)
    cv, ci = lax.top_k(cand, PEER_TOPK)
    i1 = jnp.take_along_axis(si[:, :, 0], ci // PEER_TOPK, -1)
    i2 = jnp.take_along_axis(si[:, :, 1], ci % PEER_TOPK, -1)
    e = (i1 * N_KEYS + i2).reshape(n, PEER_SLOTS).astype(jnp.int32)
    gate = jax.nn.softmax(cv, -1).reshape(n, PEER_SLOTS)
    out = peer_experts(ht, e, gate, lp['peer_u_packed'], lp['peer_v_packed'])
    return out.reshape(*lead, D_MODEL)


def layer_forward(x, p_l, pos, z_prev0, wkv0, attend, lp):
    h = rmsnorm(x, lp['attn_norm'])
    z = h @ lp['w_in']
    o_r, wkv, shift = rwkv_mix(z[..., :RWKV_COLS], z_prev0, wkv0, lp)
    qn, qr, kvc, kvs, kvw, gates = nsa_project(z[..., RWKV_COLS:RWKV_COLS + NSA_COLS], pos, lp)
    o_n = attend(qn, qr, kvc, kvs, kvw, gates)
    zg = z[..., RWKV_COLS + NSA_COLS:]
    merged = (jax.nn.sigmoid(zg[..., :D_MODEL]) * (o_r @ lp['w_rwkv_out'])
              + jax.nn.sigmoid(zg[..., D_MODEL:]) * (o_n @ lp['w_nsa_out']))
    x = x + merged @ lp['w_out']
    x = x + peer_ffn(rmsnorm(x, lp['ffn_norm']), lp)
    x = x + jax.nn.sigmoid(rmsnorm(x, lp['ple_norm']) @ lp['ple_w_gate']) * (p_l @ lp['ple_w_in'])
    return x, kvc, kvs, kvw, wkv, shift


def kernel(x_prompt, x_sample, cache_cmp_kv, cache_sel_kv, state_win_kv, state_wkv, state_shift, page_table,
           p_prompt, p_sample, attn_norm, w_in, rwkv_mu, rwkv_w0, rwkv_w_up, rwkv_a0, rwkv_a_up, rwkv_g_up,
           rwkv_k_k, rwkv_k_a, rwkv_r_k, rwkv_gn_w, rwkv_gn_b, w_rwkv_out, q_norm, k_norm, cmp_pe, cmp_w1,
           cmp_w2, w_nsa_out, w_out, ffn_norm, peer_wq, peer_keys, peer_u, peer_v, ple_norm, ple_w_gate,
           ple_w_in):
    b_p, s_p = x_prompt.shape[:2]
    pos_p = jnp.arange(s_p)
    pos_s = PAST_LEN + jnp.arange(x_sample.shape[1])
    xp, xs = x_prompt, x_sample
    l = 0
    lp = dict(attn_norm=attn_norm[l], w_in=w_in[l], rwkv_mu=rwkv_mu[l], rwkv_w0=rwkv_w0[l],
              rwkv_w_up=rwkv_w_up[l], rwkv_a0=rwkv_a0[l], rwkv_a_up=rwkv_a_up[l], rwkv_g_up=rwkv_g_up[l],
              rwkv_k_k=rwkv_k_k[l], rwkv_k_a=rwkv_k_a[l], rwkv_r_k=rwkv_r_k[l], rwkv_gn_w=rwkv_gn_w[l],
              rwkv_gn_b=rwkv_gn_b[l], w_rwkv_out=w_rwkv_out[l], q_norm=q_norm[l], k_norm=k_norm[l],
              cmp_pe=cmp_pe[l], cmp_w1=cmp_w1[l], cmp_w2=cmp_w2[l], w_nsa_out=w_nsa_out[l], w_out=w_out[l],
              ffn_norm=ffn_norm[l], peer_wq=peer_wq[l], peer_keys=peer_keys[l], peer_u=peer_u[l],
              peer_v=peer_v[l], ple_norm=ple_norm[l], ple_w_gate=ple_w_gate[l], ple_w_in=ple_w_in[l],
              peer_u_packed=pack_expert_table(peer_u[l]), peer_v_packed=pack_expert_table(peer_v[l]))
    attend_p = functools.partial(nsa_prompt, lp=lp)
    xp, kvc_p, kvs_p, kvw_p, wkv_p, sh_p = layer_forward(
        xp, p_prompt[l], pos_p, jnp.zeros((b_p, RWKV_COLS), xp.dtype),
        jnp.zeros((b_p, RWKV_HEADS, HEAD_DIM, HEAD_DIM), jnp.float32), attend_p, lp)
    win_prev = state_win_kv[l]
    attend_s = functools.partial(nsa_sample, cache_cmp=cache_cmp_kv[l], cache_sel=cache_sel_kv[l],
                                 page_table=page_table, state_win=win_prev, lp=lp)
    xs, kvc_s, kvs_s, kvw_s, wkv_s, sh_s = layer_forward(xs, p_sample[l], pos_s, state_shift[l], state_wkv[l], attend_s, lp)
    win_s = jnp.concatenate([win_prev.astype(kvw_s.dtype), kvw_s], 1)[:, -win_prev.shape[1]:]
    st = lambda a: a[None]
    return (xp, xs, st(kvc_p), st(kvc_s), st(kvs_p), st(kvs_s),
            st(kvw_p[:, -min(WINDOW, s_p):]), st(win_s), st(wkv_p), st(wkv_s), st(sh_p), st(sh_s))
```

```python
import functools
import jax, jax.numpy as jnp
from jax import lax
import numpy as np
from jax.experimental import pallas as pl
from jax.experimental.pallas import tpu as pltpu

LANES = 128
SUBLANES = 8
VMEM_TABLE_LIMIT = 48 * 1024 * 1024

D_MODEL = 1024
BATCH = 8
SEQ = 4096
DEPTH = 1
DEC_BATCH = 128
DEC_SEQ = 8
PAST_LEN = 8192
PAGE_SIZE = 128

HEAD_DIM = 64
RWKV_HEADS = 8
RWKV_WIDTH = RWKV_HEADS * HEAD_DIM
W_RANK = 64
A_RANK = 64
G_RANK = 128
RWKV_COLS = 3 * RWKV_WIDTH + W_RANK + A_RANK + G_RANK
RWKV_SPLITS = (RWKV_WIDTH, 2 * RWKV_WIDTH, 3 * RWKV_WIDTH, 3 * RWKV_WIDTH + W_RANK, 3 * RWKV_WIDTH + W_RANK + A_RANK)
GN_EPS = 64e-5
NSA_HEADS = 8
NSA_KV = 2
NSA_GROUP = NSA_HEADS // NSA_KV
NSA_WIDTH = NSA_HEADS * HEAD_DIM
KV_WIDTH = NSA_KV * HEAD_DIM
N_NSA_BRANCH = 3
NSA_COLS = NSA_WIDTH + N_NSA_BRANCH * 2 * KV_WIDTH + N_NSA_BRANCH * NSA_HEADS
CMP_BLOCK = 32
CMP_STRIDE = 16
CMP_RATIO = CMP_BLOCK // CMP_STRIDE
CMP_HIDDEN = 256
SEL_BLOCK = 64
N_SELECT = 16
WINDOW = 512
Q_BLOCK = 128
ROPE_THETA = 10000.0
IN_COLS = RWKV_COLS + NSA_COLS + 2 * D_MODEL
PEER_HEADS = 8
PEER_DK = 128
N_KEYS = 128
N_EXPERTS = N_KEYS * N_KEYS
PEER_TOPK = 16
PEER_BLOCK = 256
PLE_DIM = 256
RMS_EPS = 1e-6


def rmsnorm(x, g):
    xf = x.astype(jnp.float32)
    y = xf * lax.rsqrt(jnp.mean(xf * xf, -1, keepdims=True) + RMS_EPS)
    return (y * g).astype(x.dtype)


def rope(x, pos):
    half = HEAD_DIM // 2
    inv = ROPE_THETA ** (-jnp.arange(half, dtype=jnp.float32) / half)
    ang = pos.astype(jnp.float32)[:, None] * inv
    cos, sin = jnp.cos(ang)[:, None, :], jnp.sin(ang)[:, None, :]
    xf = x.astype(jnp.float32)
    x1, x2 = xf[..., :half], xf[..., half:]
    return jnp.concatenate([x1 * cos - x2 * sin, x2 * cos + x1 * sin], -1).astype(x.dtype)


def masked_softmax(s, mask):
    s = jnp.where(mask, s.astype(jnp.float32), -jnp.inf)
    m = jnp.max(s, -1, keepdims=True)
    m = jnp.where(jnp.isfinite(m), m, 0.0)
    e = jnp.exp(s - m)
    return e / jnp.maximum(jnp.sum(e, -1, keepdims=True), 1e-30)


WKV_GROUP = SUBLANES
WKV_CHUNK = 32
WKV_COLW = LANES


def _wkv_kernel(kk_ref, kka_ref, d_ref, dr_ref, k_ref, v_ref, c1_ref, c2_ref, s0_ref,
                o_ref, sfin_ref, s_ref, mask_ref, *, chunk):
    f32, bf16 = jnp.float32, jnp.bfloat16
    hd = 64
    c = pl.program_id(1)
    colw = mask_ref.shape[2]
    n_col = s_ref.shape[2] // colw

    @pl.when(c == 0)
    def _():
        s_ref[...] = s0_ref[0]
        lane = lax.broadcasted_iota(jnp.int32, mask_ref.shape, 2)
        row = lax.broadcasted_iota(jnp.int32, mask_ref.shape, 0)
        mask_ref[...] = ((lane % hd) == row).astype(f32)

    ones_bd = (lax.broadcasted_iota(jnp.int32, (colw, colw), 0) // hd
               == lax.broadcasted_iota(jnp.int32, (colw, colw), 1) // hd).astype(bf16)
    rows = hd * WKV_GROUP

    def seg(x3):
        x = x3.reshape(rows, colw)
        hi = x.astype(bf16)
        lo = (x - hi.astype(f32)).astype(bf16)
        y = (jnp.dot(hi, ones_bd, preferred_element_type=f32)
             + jnp.dot(lo, ones_bd, preferred_element_type=f32))
        return y.reshape(hd, WKV_GROUP, colw)

    def step(t, carry):
        kk, kka, d, dr = kk_ref[0, t], kka_ref[0, t], d_ref[0, t], dr_ref[0, t]
        k, v, c1, c2 = k_ref[0, t], v_ref[0, t], c1_ref[0, t], c2_ref[0, t]
        v_hi = v.astype(bf16).astype(f32)
        v_lo = v - v_hi
        for p in range(n_col):
            sl = slice(p * colw, (p + 1) * colw)
            m = mask_ref[...]
            s = s_ref[:, :, sl]
            sa = seg(s * kk[None, :, sl])
            tb = seg(s * dr[None, :, sl])
            vb = (jnp.dot((m * v_hi[None, :, sl]).reshape(rows, colw).astype(bf16), ones_bd, preferred_element_type=f32)
                  + jnp.dot((m * v_lo[None, :, sl]).reshape(rows, colw).astype(bf16), ones_bd, preferred_element_type=f32)
                  ).reshape(hd, WKV_GROUP, colw)
            s_ref[:, :, sl] = s * d[None, :, sl] - sa * kka[None, :, sl] + vb * k[None, :, sl]
            ob = tb - sa * c1[None, :, sl] + vb * c2[None, :, sl]
            o_ref[0, t, :, sl] = jnp.sum(ob * m, axis=0)
        return carry

    lax.fori_loop(0, chunk, step, 0)

    @pl.when(c == pl.num_programs(1) - 1)
    def _():
        sfin_ref[0] = s_ref[...]


def wkv_scan(r_h, k_h, v_h, decay, kk, a_h, wkv0):
    B, T, H, N = r_h.shape
    width = H * N
    G = B // WKV_GROUP
    chunk = min(WKV_CHUNK, T)
    kka = kk * a_h
    dr = decay * r_h
    c1 = jnp.broadcast_to(jnp.sum(kka * r_h, -1, keepdims=True), r_h.shape)
    c2 = jnp.broadcast_to(jnp.sum(k_h * r_h, -1, keepdims=True), r_h.shape)
    tm = lambda t: t.reshape(G, WKV_GROUP, T, width).transpose(0, 2, 1, 3)
    s0 = wkv0.reshape(G, WKV_GROUP, H, N, N).transpose(0, 3, 1, 2, 4).reshape(G, N, WKV_GROUP, width)
    seq_spec = pl.BlockSpec((1, chunk, WKV_GROUP, width), lambda g, c: (g, c, 0, 0))
    st_spec = pl.BlockSpec((1, N, WKV_GROUP, width), lambda g, c: (g, 0, 0, 0))
    out, s_fin = pl.pallas_call(
        functools.partial(_wkv_kernel, chunk=chunk),
        grid=(G, T // chunk),
        in_specs=[seq_spec] * 8 + [st_spec],
        out_specs=[seq_spec, st_spec],
        out_shape=[jax.ShapeDtypeStruct((G, T, WKV_GROUP, width), jnp.float32),
                   jax.ShapeDtypeStruct((G, N, WKV_GROUP, width), jnp.float32)],
        scratch_shapes=[pltpu.VMEM((N, WKV_GROUP, width), jnp.float32),
                        pltpu.VMEM((N, WKV_GROUP, WKV_COLW), jnp.float32)],
        compiler_params=pltpu.CompilerParams(dimension_semantics=("arbitrary", "arbitrary")),
        name="wkv_scan",
    )(tm(kk), tm(kka), tm(decay), tm(dr), tm(k_h), tm(v_h), tm(c1), tm(c2), s0)
    out = out.transpose(0, 2, 1, 3).reshape(B, T, H, N)
    s_fin = s_fin.reshape(G, N, WKV_GROUP, H, N).transpose(0, 2, 3, 1, 4).reshape(B, H, N, N)
    return out, s_fin


def rwkv_mix(zr, z_prev0, wkv0, lp):
    B, T, _ = zr.shape
    f32 = jnp.float32
    z_prev = jnp.concatenate([z_prev0[:, None, :].astype(zr.dtype), zr[:, :-1]], axis=1)
    xs = zr + (z_prev - zr) * lp['rwkv_mu']
    r, k, v, wl, al, gl = jnp.split(xs, RWKV_SPLITS, axis=-1)
    w = -jax.nn.softplus(-(lp['rwkv_w0'] + jnp.tanh(wl) @ lp['rwkv_w_up'])) - 0.5
    a = jax.nn.sigmoid(lp['rwkv_a0'] + al @ lp['rwkv_a_up'])
    g = jax.nn.sigmoid(gl) @ lp['rwkv_g_up']
    heads = lambda t: t.astype(f32).reshape(B, T, RWKV_HEADS, HEAD_DIM)
    kk = heads(k * lp['rwkv_k_k'])
    kk = kk / jnp.maximum(jnp.sqrt(jnp.sum(kk * kk, -1, keepdims=True)), 1e-12)
    k = k * (1.0 + (a - 1.0) * lp['rwkv_k_a'])
    r_h, k_h, v_h, a_h = heads(r), heads(k), heads(v), heads(a)
    decay = jnp.exp(-jnp.exp(heads(w)))

    out, s_fin = wkv_scan(r_h, k_h, v_h, decay, kk, a_h, wkv0.astype(f32))
    mu = jnp.mean(out, -1, keepdims=True)
    var = jnp.mean(jnp.square(out - mu), -1, keepdims=True)
    on = ((out - mu) * lax.rsqrt(var + GN_EPS)).reshape(B, T, RWKV_WIDTH) * lp['rwkv_gn_w'] + lp['rwkv_gn_b']
    bonus = jnp.sum(r_h * k_h * lp['rwkv_r_k'], -1, keepdims=True) * v_h
    o = (on + bonus.reshape(B, T, RWKV_WIDTH)) * g
    return o.astype(zr.dtype), s_fin, zr[:, -1]


def nsa_project(zn, pos, lp):
    B, T, _ = zn.shape
    q = zn[..., :NSA_WIDTH].reshape(B, T, NSA_HEADS, HEAD_DIM)
    kv = zn[..., NSA_WIDTH:NSA_WIDTH + 6 * KV_WIDTH].reshape(B, T, N_NSA_BRANCH, 2, NSA_KV, HEAD_DIM)
    gates = jax.nn.sigmoid(zn[..., NSA_WIDTH + 6 * KV_WIDTH:].reshape(B, T, NSA_HEADS, N_NSA_BRANCH))
    qn = rmsnorm(q, lp['q_norm'])
    qr = rope(qn, pos)

    def prep(kvb, gk):
        return jnp.stack([rope(rmsnorm(kvb[:, :, 0], gk), pos), kvb[:, :, 1]], axis=2)

    return qn, qr, kv[:, :, 0], prep(kv[:, :, 1], lp['k_norm'][1]), prep(kv[:, :, 2], lp['k_norm'][2]), gates


def compress(kv_raw, lp):
    n_chunk = kv_raw.shape[0] // CMP_STRIDE
    nc = n_chunk - CMP_RATIO + 1
    chunks = kv_raw[:n_chunk * CMP_STRIDE].reshape(n_chunk, CMP_STRIDE, 2, NSA_KV, HEAD_DIM)
    w1 = lp['cmp_w1'].reshape(2, CMP_RATIO, CMP_STRIDE, HEAD_DIM, CMP_HIDDEN)
    part = jnp.einsum('csekd,ersdf->crekf', chunks, w1)
    hid = part[0:nc, 0]
    for r in range(1, CMP_RATIO):
        hid = hid + part[r:r + nc, r]
    pe_bias = jnp.einsum('esd,esdf->ef', lp['cmp_pe'], lp['cmp_w1'].reshape(2, CMP_BLOCK, HEAD_DIM, CMP_HIDDEN))
    hid = jax.nn.gelu(hid + pe_bias[None, :, None, :])
    out = jnp.einsum('nekf,efd->nekd', hid, lp['cmp_w2'])
    kc = rmsnorm(out[:, 0], lp['k_norm'][0])
    return kc, out[:, 1], jnp.arange(nc) * CMP_STRIDE + CMP_BLOCK - 1


def block_cover_matrix(nsb, nc):
    c0 = jnp.arange(nc) * CMP_STRIDE
    s0 = jnp.arange(nsb) * SEL_BLOCK
    cover = (c0[None, :] <= s0[:, None] + SEL_BLOCK - 1) & (c0[None, :] + CMP_BLOCK - 1 >= s0[:, None])
    return cover.astype(jnp.float32)


def nsa_core(qn, qr, q_pos, kc, vc, c_end, nsb, npb, gather_past, kv_new, kv_win, win_pos, gates):
    tq = qn.shape[0]
    scale = HEAD_DIM ** -0.5
    qn_g = qn.reshape(tq, NSA_KV, NSA_GROUP, HEAD_DIM)
    qr_g = qr.reshape(tq, NSA_KV, NSA_GROUP, HEAD_DIM)
    causal = q_pos[:, None] >= q_pos[None, :]
    p_c = masked_softmax(jnp.einsum('tkgd,nkd->kgtn', qn_g, kc) * scale, c_end[None, :] <= q_pos[:, None])
    o_c = jnp.einsum('kgtn,nkd->tkgd', p_c, vc)
    imp = jnp.einsum('kgtn,jn->ktj', p_c, block_cover_matrix(nsb, kc.shape[0]))
    blk = jnp.arange(nsb)[None, :]
    cur = (q_pos // SEL_BLOCK)[:, None]
    forced = (blk == cur) | (blk == cur - 1) | (blk == 0)
    imp = jnp.where(forced, jnp.inf, jnp.where(blk > cur, -jnp.inf, imp))
    n_sel = min(N_SELECT, nsb)
    sel = lax.top_k(imp, n_sel)[1]
    kv_p = gather_past(sel)
    m_len = n_sel * SEL_BLOCK
    k_p = kv_p[..., 0, :].reshape(NSA_KV, tq, m_len, HEAD_DIM)
    v_p = kv_p[..., 1, :].reshape(NSA_KV, tq, m_len, HEAD_DIM)
    s_sel = jnp.concatenate([jnp.einsum('tkgd,ktmd->kgtm', qr_g, k_p),
                             jnp.einsum('tkgd,skd->kgts', qr_g, kv_new[:, 0])], -1) * scale
    m_past = jnp.repeat(sel < npb, SEL_BLOCK, axis=-1)
    member = jnp.any(sel[..., None] == (q_pos // SEL_BLOCK), axis=-2)
    p_s = masked_softmax(s_sel, jnp.concatenate([m_past, member & causal], -1)[:, None])
    o_s = (jnp.einsum('kgtm,ktmd->tkgd', p_s[..., :m_len], v_p)
           + jnp.einsum('kgts,skd->tkgd', p_s[..., m_len:], kv_new[:, 1]))
    dist = q_pos[:, None] - win_pos[None, :]
    m_w = (dist >= 0) & (dist < WINDOW) & (win_pos >= 0)[None, :]
    p_w = masked_softmax(jnp.einsum('tkgd,skd->kgts', qr_g, kv_win[:, 0]) * scale, m_w)
    o_w = jnp.einsum('kgts,skd->tkgd', p_w, kv_win[:, 1])
    g = gates.reshape(tq, NSA_KV, NSA_GROUP, N_NSA_BRANCH)
    o = o_c * g[..., 0:1] + o_s * g[..., 1:2] + o_w * g[..., 2:3]
    return o.reshape(tq, NSA_WIDTH).astype(qn.dtype)


SEL_COLS = 64
SEL_KT = 256
CHUNK_W = CMP_STRIDE * HEAD_DIM
NEG_BIG = -1e30


def _dot_nt(a, b):
    return lax.dot_general(a, b, (((1,), (1,)), ((), ())), preferred_element_type=jnp.float32)


def _split_bf16(x):
    top = lax.bitcast_convert_type(lax.bitcast_convert_type(x, jnp.uint32) & jnp.uint32(0xFFFF0000), jnp.float32)
    return top.astype(jnp.bfloat16), (x - top).astype(jnp.bfloat16)


def _dot3(a, b_hi, b_lo):
    a_hi, a_lo = _split_bf16(a)
    f32 = jnp.float32
    return (jnp.dot(a_hi, b_hi, preferred_element_type=f32) + jnp.dot(a_lo, b_hi, preferred_element_type=f32)
            + jnp.dot(a_hi, b_lo, preferred_element_type=f32))


def _compress_kernel(c_ref, cn_ref, w1h_ref, w1l_ref, pe_ref, w2h_ref, w2l_ref, kn_ref, kc_ref, vc_ref):
    for e in range(2):
        w1h, w1l = w1h_ref[e], w1l_ref[e]
        hid = (_dot3(c_ref[0, e, 0], w1h[:CHUNK_W], w1l[:CHUNK_W])
               + _dot3(cn_ref[0, e, 0], w1h[CHUNK_W:], w1l[CHUNK_W:]))
        pe_bias = _dot3(pe_ref[e], w1h, w1l)[0:1]
        hid = jax.nn.gelu(hid + pe_bias)
        out = _dot3(hid, w2h_ref[e], w2l_ref[e])
        if e == 0:
            out = out * lax.rsqrt(jnp.mean(out * out, -1, keepdims=True) + RMS_EPS) * kn_ref[...]
            kc_ref[0, 0] = out
        else:
            vc_ref[0, 0] = out


def compress_prompt(kv_cmp, lp):
    B, S = kv_cmp.shape[:2]
    nch = S // CMP_STRIDE
    c = kv_cmp.transpose(0, 2, 3, 1, 4).reshape(B, 2, NSA_KV, nch, CHUNK_W)
    cn = jnp.concatenate([c[:, :, :, 1:], jnp.zeros_like(c[:, :, :, :1])], axis=3)
    pe = jnp.broadcast_to(lp['cmp_pe'].reshape(2, 1, CMP_BLOCK * HEAD_DIM), (2, SUBLANES, CMP_BLOCK * HEAD_DIM))
    c_spec = pl.BlockSpec((1, 2, 1, nch, CHUNK_W), lambda b, k: (b, 0, k, 0, 0))
    full = lambda shape: pl.BlockSpec(shape, lambda b, k: (0,) * len(shape))
    o_spec = pl.BlockSpec((1, 1, nch, HEAD_DIM), lambda b, k: (b, k, 0, 0))
    w1_spec = full((2, CMP_BLOCK * HEAD_DIM, CMP_HIDDEN))
    w2_spec = full((2, CMP_HIDDEN, HEAD_DIM))
    return pl.pallas_call(
        _compress_kernel,
        grid=(B, NSA_KV),
        in_specs=[c_spec, c_spec, w1_spec, w1_spec, full((2, SUBLANES, CMP_BLOCK * HEAD_DIM)),
                  w2_spec, w2_spec, full((1, HEAD_DIM))],
        out_specs=[o_spec, o_spec],
        out_shape=[jax.ShapeDtypeStruct((B, NSA_KV, nch, HEAD_DIM), jnp.float32)] * 2,
        compiler_params=pltpu.CompilerParams(dimension_semantics=("arbitrary", "arbitrary")),
        name="nsa_compress",
    )(c, cn, *_split_bf16(lp['cmp_w1']), pe, *_split_bf16(lp['cmp_w2']), lp['k_norm'][0:1])


def _flash_update(q, kt, vt, valid, m, l, acc, scale):
    s = jnp.where(valid, _dot_nt(q, kt) * scale, NEG_BIG)
    m_new = jnp.maximum(m, jnp.max(s, -1, keepdims=True))
    p = jnp.where(valid, jnp.exp(s - m_new), 0.0)
    alpha = jnp.exp(m - m_new)
    l = alpha * l + jnp.sum(p, -1, keepdims=True)
    acc = alpha * acc + jnp.dot(p.astype(jnp.bfloat16), vt, preferred_element_type=jnp.float32)
    return m_new, l, acc


def _nsa_prompt_kernel(qn_ref, qr_ref, kc_ref, vc_ref, ks_ref, vs_ref, kw_ref, vw_ref, g_ref, o_ref):
    f32, bf16 = jnp.float32, jnp.bfloat16
    i = pl.program_id(2)
    t0 = i * Q_BLOCK
    scale = HEAD_DIM ** -0.5
    nch = kc_ref.shape[2]
    t_col = t0 + lax.broadcasted_iota(jnp.int32, (Q_BLOCK, 1), 0)

    kc = kc_ref[0, 0].astype(bf16)
    vc = vc_ref[0, 0].astype(bf16)
    n_row = lax.broadcasted_iota(jnp.int32, (1, nch), 1)
    c_valid = (n_row * CMP_STRIDE + (CMP_BLOCK - 1) <= t_col) & (n_row < nch - 1)
    p_sum = jnp.zeros((Q_BLOCK, nch), f32)
    o_cmp = []
    for g in range(NSA_GROUP):
        s = jnp.where(c_valid, _dot_nt(qn_ref[0, 0, g].astype(bf16), kc) * scale, -jnp.inf)
        m = jnp.max(s, -1, keepdims=True)
        m = jnp.where(m > -jnp.inf, m, 0.0)
        e = jnp.exp(s - m)
        p = e / jnp.maximum(jnp.sum(e, -1, keepdims=True), 1e-30)
        p_sum = p_sum + p
        o_cmp.append(jnp.dot(p.astype(bf16), vc, preferred_element_type=f32))

    n_col = lax.broadcasted_iota(jnp.int32, (nch, SEL_COLS), 0) * CMP_STRIDE
    j_blk = lax.broadcasted_iota(jnp.int32, (nch, SEL_COLS), 1) * SEL_BLOCK
    cover = ((n_col <= j_blk + (SEL_BLOCK - 1)) & (n_col + (CMP_BLOCK - 1) >= j_blk)).astype(bf16)
    p_hi = p_sum.astype(bf16)
    p_lo = (p_sum - p_hi.astype(f32)).astype(bf16)
    imp = jnp.dot(p_hi, cover, preferred_element_type=f32) + jnp.dot(p_lo, cover, preferred_element_type=f32)
    j_row = lax.broadcasted_iota(jnp.int32, (1, SEL_COLS), 1)
    cur = t_col // SEL_BLOCK
    forced = (j_row == cur) | (j_row == cur - 1) | (j_row == 0)
    imp = jnp.where(forced, jnp.inf, jnp.where(j_row > cur, -jnp.inf, imp))
    rank = jnp.zeros((Q_BLOCK, SEL_COLS), f32)
    for jp in range(SEL_COLS):
        col = imp[:, jp:jp + 1]
        rank = rank + jnp.where((col > imp) | ((col == imp) & (jp < j_row)), 1.0, 0.0)
    sel_mask = jnp.where(rank < N_SELECT, 1.0, 0.0).astype(bf16)

    q_rot = [qr_ref[0, 0, g].astype(bf16) for g in range(NSA_GROUP)]

    def init():
        return tuple((jnp.full((Q_BLOCK, 1), NEG_BIG, f32), jnp.zeros((Q_BLOCK, 1), f32),
                      jnp.zeros((Q_BLOCK, HEAD_DIM), f32)) for _ in range(NSA_GROUP))

    def sel_body(c, carry):
        k0 = pl.multiple_of(c * SEL_KT, SEL_KT)
        kt = ks_ref[0, 0, pl.ds(k0, SEL_KT), :].astype(bf16)
        vt = vs_ref[0, 0, pl.ds(k0, SEL_KT), :].astype(bf16)
        kpos = k0 + lax.broadcasted_iota(jnp.int32, (1, SEL_KT), 1)
        expand = (lax.broadcasted_iota(jnp.int32, (SEL_COLS, SEL_KT), 0)
                  == (k0 + lax.broadcasted_iota(jnp.int32, (SEL_COLS, SEL_KT), 1)) // SEL_BLOCK).astype(bf16)
        valid = (jnp.dot(sel_mask, expand, preferred_element_type=f32) > 0.5) & (kpos <= t_col)
        return tuple(_flash_update(q_rot[g], kt, vt, valid, *carry[g], scale) for g in range(NSA_GROUP))

    sel = lax.fori_loop(0, i // (SEL_KT // Q_BLOCK) + 1, sel_body, init())

    def win_body(c, carry):
        k0 = pl.multiple_of(c * SEL_KT, SEL_KT)
        kt = kw_ref[0, 0, pl.ds(k0, SEL_KT), :].astype(bf16)
        vt = vw_ref[0, 0, pl.ds(k0, SEL_KT), :].astype(bf16)
        dist = t_col - (k0 + lax.broadcasted_iota(jnp.int32, (1, SEL_KT), 1))
        valid = (dist >= 0) & (dist < WINDOW)
        return tuple(_flash_update(q_rot[g], kt, vt, valid, *carry[g], scale) for g in range(NSA_GROUP))

    last_tile = i // (SEL_KT // Q_BLOCK)
    win = lax.fori_loop(jnp.maximum(last_tile - WINDOW // SEL_KT, 0), last_tile + 1, win_body, init())

    for g in range(NSA_GROUP):
        gate = g_ref[0, 0, g]
        o_sel = sel[g][2] / jnp.maximum(sel[g][1], 1e-30)
        o_win = win[g][2] / jnp.maximum(win[g][1], 1e-30)
        o_ref[0, 0, g] = o_cmp[g] * gate[:, 0:1] + o_sel * gate[:, 1:2] + o_win * gate[:, 2:3]


def nsa_prompt(qn, qr, kv_cmp, kv_sel, kv_win, gates, lp):
    B, S = qn.shape[:2]
    assert S % SEL_KT == 0 and S <= SEL_COLS * SEL_BLOCK
    nch = S // CMP_STRIDE
    kc, vc = compress_prompt(kv_cmp, lp)
    heads = lambda t: t.reshape(B, S, NSA_KV, NSA_GROUP, t.shape[-1]).transpose(0, 2, 3, 1, 4)
    kv_major = lambda t: t.transpose(0, 2, 1, 3)
    q_spec = pl.BlockSpec((1, 1, NSA_GROUP, Q_BLOCK, HEAD_DIM), lambda b, k, i: (b, k, 0, i, 0))
    g_spec = pl.BlockSpec((1, 1, NSA_GROUP, Q_BLOCK, N_NSA_BRANCH), lambda b, k, i: (b, k, 0, i, 0))
    c_spec = pl.BlockSpec((1, 1, nch, HEAD_DIM), lambda b, k, i: (b, k, 0, 0))
    s_spec = pl.BlockSpec((1, 1, S, HEAD_DIM), lambda b, k, i: (b, k, 0, 0))
    o = pl.pallas_call(
        _nsa_prompt_kernel,
        grid=(B, NSA_KV, S // Q_BLOCK),
        in_specs=[q_spec, q_spec, c_spec, c_spec, s_spec, s_spec, s_spec, s_spec, g_spec],
        out_specs=q_spec,
        out_shape=jax.ShapeDtypeStruct((B, NSA_KV, NSA_GROUP, S, HEAD_DIM), jnp.float32),
        compiler_params=pltpu.CompilerParams(dimension_semantics=("arbitrary", "arbitrary", "arbitrary")),
        name="nsa_prompt",
    )(heads(qn), heads(qr), kc, vc, kv_major(kv_sel[:, :, 0]), kv_major(kv_sel[:, :, 1]),
      kv_major(kv_win[:, :, 0]), kv_major(kv_win[:, :, 1]), heads(gates))
    return o.transpose(0, 3, 1, 2, 4).reshape(B, S, NSA_WIDTH)


PAGE_GROUP = 16
PAGE_CHUNKS = PAGE_SIZE // CMP_STRIDE
KV_ROW = 2 * KV_WIDTH
SAMPLE_ROWS = NSA_HEADS * DEC_SEQ
SEL_COLS_S = 256


def _page_specs(block, n_pages):
    zeros = (0,) * (len(block) - 1)
    return [pl.BlockSpec(block, (lambda b, c, pt, r=r: (pt[b * n_pages + c * PAGE_GROUP + r],) + zeros))
            for r in range(PAGE_GROUP)]


def _compress_sample_kernel(pt_ref, *refs):
    f32, bf16 = jnp.float32, jnp.bfloat16
    pages = refs[:PAGE_GROUP]
    w1h_ref, w1l_ref, pe_ref, w2h_ref, w2l_ref, kn_ref, kc_ref, vc_ref, p0_sc, p1_sc = refs[PAGE_GROUP:]
    c = pl.program_id(1)
    rows = PAGE_GROUP * PAGE_CHUNKS
    r0 = pl.multiple_of(c * rows, rows)
    for e in range(2):
        cm = jnp.concatenate([pg[0, e, k] for k in range(NSA_KV) for pg in pages], axis=0)
        c_hi, c_lo = _split_bf16(cm)
        for part, dst in enumerate((p0_sc, p1_sc)):
            w_hi = w1h_ref[e, part * CHUNK_W:(part + 1) * CHUNK_W, :]
            w_lo = w1l_ref[e, part * CHUNK_W:(part + 1) * CHUNK_W, :]
            y = (jnp.dot(c_hi, w_hi, preferred_element_type=f32) + jnp.dot(c_lo, w_hi, preferred_element_type=f32)
                 + jnp.dot(c_hi, w_lo, preferred_element_type=f32))
            for k in range(NSA_KV):
                dst[e, k, pl.ds(r0, rows), :] = y[k * rows:(k + 1) * rows]

    @pl.when(c == pl.num_programs(1) - 1)
    def _():
        nch = p0_sc.shape[2]
        for e in range(2):
            pe_bias = _dot3(pe_ref[e], w1h_ref[e], w1l_ref[e])[0:1]
            out = jnp.zeros((nch, LANES), f32)
            for k in range(NSA_KV):
                nxt = pltpu.roll(p1_sc[e, k], nch - 1, 0)
                hid = jax.nn.gelu(p0_sc[e, k] + nxt + pe_bias)
                out = out + _dot3(hid, w2h_ref[e, k], w2l_ref[e, k])
            if e == 0:
                lane = lax.broadcasted_iota(jnp.int32, (1, LANES), 1)
                sq = out * out
                ms0 = jnp.sum(jnp.where(lane < HEAD_DIM, sq, 0.0), -1, keepdims=True) / HEAD_DIM
                ms1 = jnp.sum(jnp.where(lane >= HEAD_DIM, sq, 0.0), -1, keepdims=True) / HEAD_DIM
                kc_ref[0] = out * jnp.where(lane < HEAD_DIM, lax.rsqrt(ms0 + RMS_EPS), lax.rsqrt(ms1 + RMS_EPS)) * kn_ref[...]
            else:
                vc_ref[0] = out


def _nsa_sample_kernel(pt_ref, *refs):
    f32, bf16 = jnp.float32, jnp.bfloat16
    pages = refs[:PAGE_GROUP]
    (qn_ref, qr_ref, kc_ref, vc_ref, knew_ref, win_ref, g_ref, o_ref,
     m_sc, l_sc, acc_sc, selm_sc, ocmp_sc) = refs[PAGE_GROUP:]
    c = pl.program_id(1)
    scale = HEAD_DIM ** -0.5
    T = DEC_SEQ
    nch = kc_ref.shape[1]
    keys = PAGE_GROUP * PAGE_SIZE
    t_row = lax.broadcasted_iota(jnp.int32, (SAMPLE_ROWS, 1), 0) % T
    q_rot = qr_ref[0].astype(bf16)

    @pl.when(c == 0)
    def _():
        n_row = lax.broadcasted_iota(jnp.int32, (1, nch), 1)
        s = jnp.where(n_row < nch - 1, _dot_nt(qn_ref[0].astype(bf16), kc_ref[0].astype(bf16)) * scale, -jnp.inf)
        e = jnp.exp(s - jnp.max(s, -1, keepdims=True))
        p = e / jnp.maximum(jnp.sum(e, -1, keepdims=True), 1e-30)
        ocmp_sc[...] = jnp.dot(p.astype(bf16), vc_ref[0].astype(bf16), preferred_element_type=f32)
        grp = NSA_GROUP * T
        p_sum = jnp.concatenate(
            [sum(p[k * grp + g * T:k * grp + (g + 1) * T] for g in range(NSA_GROUP)) for k in range(NSA_KV)], axis=0)
        n_col = lax.broadcasted_iota(jnp.int32, (nch, SEL_COLS_S), 0) * CMP_STRIDE
        j_blk = lax.broadcasted_iota(jnp.int32, (nch, SEL_COLS_S), 1) * SEL_BLOCK
        cover = ((n_col <= j_blk + (SEL_BLOCK - 1)) & (n_col + (CMP_BLOCK - 1) >= j_blk)).astype(bf16)
        p_hi = p_sum.astype(bf16)
        p_lo = (p_sum - p_hi.astype(f32)).astype(bf16)
        imp = jnp.dot(p_hi, cover, preferred_element_type=f32) + jnp.dot(p_lo, cover, preferred_element_type=f32)
        j_row = lax.broadcasted_iota(jnp.int32, (1, SEL_COLS_S), 1)
        cur = (PAST_LEN + lax.broadcasted_iota(jnp.int32, (NSA_KV * T, 1), 0) % T) // SEL_BLOCK
        forced = (j_row == cur) | (j_row == cur - 1) | (j_row == 0)
        imp = jnp.where(forced, jnp.inf, jnp.where(j_row > cur, -jnp.inf, imp))
        rank = jnp.zeros((NSA_KV * T, SEL_COLS_S), f32)
        for jp in range(-(-(PAST_LEN + T) // SEL_BLOCK)):
            col = imp[:, jp:jp + 1]
            rank = rank + jnp.where((col > imp) | ((col == imp) & (jp < j_row)), 1.0, 0.0)
        sel16 = jnp.where(rank < N_SELECT, 1.0, 0.0)
        selm_sc[...] = jnp.concatenate([sel16[k * T:(k + 1) * T] for k in range(NSA_KV) for _ in range(NSA_GROUP)], axis=0)
        m_sc[...] = jnp.full(m_sc.shape, NEG_BIG, f32)
        l_sc[...] = jnp.zeros(l_sc.shape, f32)
        acc_sc[...] = jnp.zeros(acc_sc.shape, f32)

    def update(kt, vt, valid):
        m, l, acc = _flash_update(q_rot, kt, vt, valid, m_sc[:, 0:1], l_sc[:, 0:1], acc_sc[...], scale)
        m_sc[...] = jnp.broadcast_to(m, m_sc.shape)
        l_sc[...] = jnp.broadcast_to(l, l_sc.shape)
        acc_sc[...] = acc

    kt = jnp.concatenate([pg[0, :, 0:KV_WIDTH] for pg in pages], axis=0).astype(bf16)
    vt = jnp.concatenate([pg[0, :, KV_WIDTH:KV_ROW] for pg in pages], axis=0).astype(bf16)
    past_cols = PAST_LEN // SEL_BLOCK
    expand = (lax.broadcasted_iota(jnp.int32, (past_cols, keys), 0)
              == c * (keys // SEL_BLOCK) + lax.broadcasted_iota(jnp.int32, (past_cols, keys), 1) // SEL_BLOCK).astype(bf16)
    update(kt, vt, jnp.dot(selm_sc[:, 0:past_cols].astype(bf16), expand, preferred_element_type=f32) > 0.5)

    @pl.when(c == pl.num_programs(1) - 1)
    def _():
        s_idx = lax.broadcasted_iota(jnp.int32, (1, LANES), 1)
        member = selm_sc[:, past_cols:past_cols + 1] > 0.5
        update(knew_ref[0, :, 0:KV_WIDTH].astype(bf16), knew_ref[0, :, KV_WIDTH:KV_ROW].astype(bf16),
               member & (s_idx <= t_row) & (s_idx < T))
        o_sel = acc_sc[...] / jnp.maximum(l_sc[:, 0:1], 1e-30)
        n_win = win_ref.shape[1]
        win_cache = min(WINDOW, PAST_LEN)
        w_idx = lax.broadcasted_iota(jnp.int32, (1, n_win), 1)
        dist = (PAST_LEN + t_row) - (PAST_LEN - win_cache + w_idx)
        valid = (dist >= 0) & (dist < WINDOW) & (w_idx < win_cache + T)
        m, l, acc = _flash_update(q_rot, win_ref[0, :, 0:KV_WIDTH].astype(bf16), win_ref[0, :, KV_WIDTH:KV_ROW].astype(bf16),
                                  valid, jnp.full((SAMPLE_ROWS, 1), NEG_BIG, f32), jnp.zeros((SAMPLE_ROWS, 1), f32),
                                  jnp.zeros((SAMPLE_ROWS, LANES), f32), scale)
        o_win = acc / jnp.maximum(l, 1e-30)
        o = (ocmp_sc[...] * g_ref[0, :, 0:LANES] + o_sel * g_ref[0, :, LANES:2 * LANES]
             + o_win * g_ref[0, :, 2 * LANES:3 * LANES])
        first = lax.broadcasted_iota(jnp.int32, (SAMPLE_ROWS, 1), 0) < SAMPLE_ROWS // NSA_KV
        o_ref[0] = jnp.where(first, o, pltpu.roll(o, HEAD_DIM, 1))


def nsa_sample(qn, qr, kv_cmp, kv_sel, kv_win, gates, cache_cmp, cache_sel, page_table, state_win, lp):
    B, T = qn.shape[:2]
    n_pages = PAST_LEN // PAGE_SIZE
    n_phys = cache_cmp.shape[0]
    nch = PAST_LEN // CMP_STRIDE
    assert T == DEC_SEQ == SUBLANES and (PAST_LEN + T) // CMP_STRIDE == nch and n_pages % PAGE_GROUP == 0
    assert -(-(PAST_LEN + T) // SEL_BLOCK) <= SEL_COLS_S
    pt = page_table.reshape(B * n_pages)
    n_steps = n_pages // PAGE_GROUP
    const = lambda shape: pl.BlockSpec(shape, lambda b, c, pt: (0,) * len(shape))
    per_seq = lambda shape: pl.BlockSpec((1,) + shape, lambda b, c, pt: (b,) + (0,) * len(shape))
    params = pltpu.CompilerParams(dimension_semantics=("arbitrary", "arbitrary"))

    cmp_pages = cache_cmp.reshape(n_phys, PAGE_CHUNKS, CMP_STRIDE, 2, NSA_KV, HEAD_DIM).transpose(0, 3, 4, 1, 2, 5)
    cmp_pages = cmp_pages.reshape(n_phys, 2, NSA_KV, PAGE_CHUNKS, CHUNK_W)
    pe = jnp.broadcast_to(lp['cmp_pe'].reshape(2, 1, CMP_BLOCK * HEAD_DIM), (2, SUBLANES, CMP_BLOCK * HEAD_DIM))
    w2 = lp['cmp_w2']
    zero = jnp.zeros_like(w2)
    w2_half = jnp.stack([jnp.concatenate([w2, zero], -1), jnp.concatenate([zero, w2], -1)], axis=1)
    kn = jnp.tile(lp['k_norm'][0:1], (1, NSA_KV))
    kc, vc = pl.pallas_call(
        _compress_sample_kernel,
        grid_spec=pltpu.PrefetchScalarGridSpec(
            num_scalar_prefetch=1, grid=(B, n_steps),
            in_specs=_page_specs((1, 2, NSA_KV, PAGE_CHUNKS, CHUNK_W), n_pages)
            + [const((2, CMP_BLOCK * HEAD_DIM, CMP_HIDDEN))] * 2 + [const((2, SUBLANES, CMP_BLOCK * HEAD_DIM))]
            + [const((2, NSA_KV, CMP_HIDDEN, LANES))] * 2 + [const((1, LANES))],
            out_specs=[per_seq((nch, LANES)), per_seq((nch, LANES))],
            scratch_shapes=[pltpu.VMEM((2, NSA_KV, nch, CMP_HIDDEN), jnp.float32)] * 2),
        out_shape=[jax.ShapeDtypeStruct((B, nch, LANES), jnp.float32)] * 2,
        compiler_params=params,
        name="nsa_compress_sample",
    )(pt, *([cmp_pages] * PAGE_GROUP), *_split_bf16(lp['cmp_w1']), pe, *_split_bf16(w2_half), kn)

    def rows(t):
        return t.reshape(B, T, NSA_KV, NSA_GROUP, t.shape[-1]).transpose(0, 2, 3, 1, 4).reshape(B, SAMPLE_ROWS, t.shape[-1])

    def lane_halves(q):
        first = (jnp.arange(SAMPLE_ROWS) < SAMPLE_ROWS // NSA_KV)[None, :, None]
        return jnp.concatenate([jnp.where(first, q, 0.0), jnp.where(first, 0.0, q)], -1)

    flat = lambda t: t.reshape(t.shape[0], t.shape[1], KV_ROW)
    pad_rows = lambda t, n: jnp.pad(t, ((0, 0), (0, n - t.shape[1]), (0, 0)))
    k_new = pad_rows(flat(kv_sel), LANES)
    win_cache = state_win.shape[1]
    n_win = -(-(win_cache + T) // LANES) * LANES
    win_all = pad_rows(jnp.concatenate([flat(state_win), flat(kv_win)], 1), n_win)
    sel_pages = cache_sel.reshape(n_phys, PAGE_SIZE, KV_ROW)
    o = pl.pallas_call(
        _nsa_sample_kernel,
        grid_spec=pltpu.PrefetchScalarGridSpec(
            num_scalar_prefetch=1, grid=(B, n_steps),
            in_specs=_page_specs((1, PAGE_SIZE, KV_ROW), n_pages)
            + [per_seq((SAMPLE_ROWS, LANES)), per_seq((SAMPLE_ROWS, LANES)), per_seq((nch, LANES)), per_seq((nch, LANES)),
               per_seq((LANES, KV_ROW)), per_seq((n_win, KV_ROW)), per_seq((SAMPLE_ROWS, N_NSA_BRANCH * LANES))],
            out_specs=per_seq((SAMPLE_ROWS, LANES)),
            scratch_shapes=[pltpu.VMEM((SAMPLE_ROWS, LANES), jnp.float32)] * 3
            + [pltpu.VMEM((SAMPLE_ROWS, SEL_COLS_S), jnp.float32), pltpu.VMEM((SAMPLE_ROWS, LANES), jnp.float32)]),
        out_shape=jax.ShapeDtypeStruct((B, SAMPLE_ROWS, LANES), jnp.float32),
        compiler_params=params,
        name="nsa_sample",
    )(pt, *([sel_pages] * PAGE_GROUP), lane_halves(rows(qn)), lane_halves(rows(qr)), kc, vc, k_new, win_all,
      jnp.repeat(rows(gates), LANES, axis=-1))
    o = o[:, :, :HEAD_DIM]
    return o.reshape(B, NSA_KV, NSA_GROUP, T, HEAD_DIM).transpose(0, 3, 1, 2, 4).reshape(B, T, NSA_WIDTH)


PEER_SLOTS = PEER_HEADS * PEER_TOPK
PEER_TOK = 128
HALF_D = D_MODEL // 2
ROW_SUB = HALF_D // LANES


def pack_expert_table(tab):
    bits = lax.bitcast_convert_type(tab.astype(jnp.bfloat16), jnp.uint16).astype(jnp.uint32)
    return ((bits[:, HALF_D:] << 16) | bits[:, :HALF_D]).reshape(tab.shape[0], ROW_SUB, LANES)


def _unpack_row(row):
    lo = lax.bitcast_convert_type(row << 16, jnp.float32)
    hi = lax.bitcast_convert_type(row & jnp.uint32(0xFFFF0000), jnp.float32)
    return lo, hi


def _load_table_and_indices(idx_hbm, tab_hbm, tab_vmem, idx_smem, sem):
    i = pl.program_id(0)

    @pl.when(i == 0)
    def _():
        cp = pltpu.make_async_copy(tab_hbm, tab_vmem, sem.at[0])
        cp.start()
        cp.wait()

    n = idx_smem.shape[0]
    cp = pltpu.make_async_copy(idx_hbm.at[pl.ds(i * n, n)], idx_smem, sem.at[1])
    cp.start()
    cp.wait()


def _eye():
    return (lax.broadcasted_iota(jnp.int32, (PEER_SLOTS, LANES), 0)
            == lax.broadcasted_iota(jnp.int32, (PEER_SLOTS, LANES), 1))


def _peer_act_kernel(idx_hbm, h_ref, gate_ref, tab_hbm, w_ref, tab_vmem, idx_smem, ps_ref, a_ref, sem):
    _load_table_and_indices(idx_hbm, tab_hbm, tab_vmem, idx_smem, sem)
    eye = _eye()

    def tok(t, carry):
        h_lo = h_ref[t, 0:ROW_SUB, :]
        h_hi = h_ref[t, ROW_SUB:2 * ROW_SUB, :]
        for j in range(PEER_SLOTS):
            lo, hi = _unpack_row(tab_vmem[idx_smem[t * PEER_SLOTS + j]])
            ps_ref[j * ROW_SUB:(j + 1) * ROW_SUB, :] = lo * h_lo + hi * h_hi
        q = ps_ref[pl.ds(0, PEER_SLOTS, stride=ROW_SUB), :]
        for s in range(1, ROW_SUB):
            q = q + ps_ref[pl.ds(s, PEER_SLOTS, stride=ROW_SUB), :]
        a_col = jnp.sum(q, axis=-1, keepdims=True)
        a_ref[pl.ds(t, 1), :] = jnp.sum(jnp.where(eye, a_col, 0.0), axis=0, keepdims=True)
        return carry

    lax.fori_loop(0, h_ref.shape[0], tok, 0)
    w_ref[...] = gate_ref[...] * jax.nn.gelu(a_ref[...])


def _peer_out_kernel(idx_hbm, w_ref, tab_hbm, o_ref, tab_vmem, idx_smem, wb_ref, sem):
    _load_table_and_indices(idx_hbm, tab_hbm, tab_vmem, idx_smem, sem)
    eye = _eye()
    n_acc = 4

    def tok(t, carry):
        w_row = w_ref[pl.ds(t, 1), :]
        w_col = jnp.sum(jnp.where(eye, w_row, 0.0), axis=-1, keepdims=True)
        wb_ref[...] = jnp.broadcast_to(w_col, (PEER_SLOTS, LANES))
        acc_lo = [jnp.zeros((ROW_SUB, LANES), jnp.float32) for _ in range(n_acc)]
        acc_hi = [jnp.zeros((ROW_SUB, LANES), jnp.float32) for _ in range(n_acc)]
        for j in range(PEER_SLOTS):
            lo, hi = _unpack_row(tab_vmem[idx_smem[t * PEER_SLOTS + j]])
            wj = wb_ref[j:j + 1, :]
            acc_lo[j % n_acc] = acc_lo[j % n_acc] + wj * lo
            acc_hi[j % n_acc] = acc_hi[j % n_acc] + wj * hi
        o_ref[t, 0:ROW_SUB, :] = (acc_lo[0] + acc_lo[1]) + (acc_lo[2] + acc_lo[3])
        o_ref[t, ROW_SUB:2 * ROW_SUB, :] = (acc_hi[0] + acc_hi[1]) + (acc_hi[2] + acc_hi[3])
        return carry

    lax.fori_loop(0, w_ref.shape[0], tok, 0)


def peer_experts(ht, e, gate, u_packed, v_packed):
    n = ht.shape[0]
    n_exp = u_packed.shape[0]
    tb = min(PEER_TOK, n)
    assert n % tb == 0 and tb % SUBLANES == 0
    idx =e.reshape(n * PEER_SLOTS)
    h3 = ht.reshape(n, 2 * ROW_SUB, LANES)
    any_spec = pl.BlockSpec(memory_space=pl.ANY)
    tok_spec = pl.BlockSpec((tb, PEER_SLOTS), lambda i: (i, 0))
    row_spec = pl.BlockSpec((tb, 2 * ROW_SUB, LANES), lambda i: (i, 0, 0))
    params = pltpu.CompilerParams(dimension_semantics=("arbitrary",), vmem_limit_bytes=VMEM_TABLE_LIMIT)
    table = pltpu.VMEM((n_exp, ROW_SUB, LANES), jnp.uint32)
    idx_smem = pltpu.SMEM((tb * PEER_SLOTS,), jnp.int32)
    w = pl.pallas_call(
        _peer_act_kernel,
        grid=(n // tb,),
        in_specs=[any_spec, row_spec, tok_spec, any_spec],
        out_specs=tok_spec,
        out_shape=jax.ShapeDtypeStruct((n, PEER_SLOTS), jnp.float32),
        scratch_shapes=[table, idx_smem, pltpu.VMEM((PEER_SLOTS * ROW_SUB, LANES), jnp.float32),
                        pltpu.VMEM((tb, PEER_SLOTS), jnp.float32), pltpu.SemaphoreType.DMA((2,))],
        compiler_params=params,
        name="peer_act",
    )(idx, h3, gate, u_packed)
    out = pl.pallas_call(
        _peer_out_kernel,
        grid=(n // tb,),
        in_specs=[any_spec, tok_spec, any_spec],
        out_specs=row_spec,
        out_shape=jax.ShapeDtypeStruct((n, 2 * ROW_SUB, LANES), jnp.float32),
        scratch_shapes=[table, idx_smem, pltpu.VMEM((PEER_SLOTS, LANES), jnp.float32),
                        pltpu.SemaphoreType.DMA((2,))],
        compiler_params=params,
        name="peer_out",
    )(idx, w, v_packed)
    return out.reshape(n, D_MODEL)


def _top_rows(s, row, n_top, val_sc, idx_sc, base):
    for r in range(n_top):
        m = jnp.max(s, axis=0, keepdims=True)
        idx = jnp.min(jnp.where(s == m, row, float(s.shape[0])), axis=0, keepdims=True)
        s = jnp.where(row == idx, -jnp.inf, s)
        val_sc[pl.ds(base + r, 1), :] = m
        idx_sc[pl.ds(base + r, 1), :] = idx


def _peer_route_kernel(h_ref, wq_ref, khi_ref, klo_ref, e_ref, g_ref, sv_sc, si_sc, cv_sc, ci_sc, eo_sc, go_sc):
    f32, bf16 = jnp.float32, jnp.bfloat16
    tb = h_ref.shape[0]
    k = PEER_TOPK
    h_bf = h_ref[...].astype(bf16)
    key_row = lax.broadcasted_iota(jnp.int32, (N_KEYS, tb), 0).astype(f32)
    pair_row = lax.broadcasted_iota(jnp.int32, (k * k, tb), 0).astype(f32)

    def head(hh, carry):
        col = pl.multiple_of(hh * PEER_DK, PEER_DK)
        q = jnp.dot(h_bf, wq_ref[:, pl.ds(col, PEER_DK)], preferred_element_type=f32)
        q_hi, q_lo = _split_bf16(q)
        for c in range(2):
            k_hi, k_lo = khi_ref[hh, c], klo_ref[hh, c]
            s = _dot_nt(k_hi, q_hi) + _dot_nt(k_lo, q_hi) + _dot_nt(k_hi, q_lo)
            _top_rows(s, key_row, k, sv_sc, si_sc, c * k)
        sv0, sv1 = sv_sc[0:k, :], sv_sc[k:2 * k, :]
        si0, si1 = si_sc[0:k, :], si_sc[k:2 * k, :]
        cand = jnp.concatenate([sv0[a:a + 1, :] + sv1 for a in range(k)], axis=0)
        ids = jnp.concatenate([si0[a:a + 1, :] * float(N_KEYS) + si1 for a in range(k)], axis=0)
        _top_rows(cand, pair_row, k, cv_sc, ci_sc, 0)
        ci = ci_sc[...]
        base = pl.multiple_of(hh * k, k)
        for r in range(k):
            eo_sc[pl.ds(base + r, 1), :] = jnp.sum(jnp.where(pair_row == ci[r:r + 1, :], ids, 0.0), axis=0, keepdims=True)
        ex = jnp.exp(cv_sc[...] - cv_sc[0:1, :])
        go_sc[pl.ds(base, k), :] = ex / jnp.sum(ex, axis=0, keepdims=True)
        return carry

    lax.fori_loop(0, PEER_HEADS, head, 0)
    e_ref[...] = eo_sc[...].T.astype(jnp.int32)
    g_ref[...] = go_sc[...].T


def peer_route(ht, lp):
    n = ht.shape[0]
    tb = min(PEER_TOK, n)
    assert n % tb == 0 and tb == PEER_SLOTS
    half = PEER_DK // 2
    keys = lp['peer_keys']
    zero = jnp.zeros_like(keys[:, 0])
    keys = jnp.stack([jnp.concatenate([keys[:, 0], zero], -1), jnp.concatenate([zero, keys[:, 1]], -1)], axis=1)
    k_hi, k_lo = _split_bf16(keys)
    const = lambda shape: pl.BlockSpec(shape, lambda i: (0,) * len(shape))
    tok_spec = pl.BlockSpec((tb, PEER_SLOTS), lambda i: (i, 0))
    return pl.pallas_call(
        _peer_route_kernel,
        grid=(n // tb,),
        in_specs=[pl.BlockSpec((tb, D_MODEL), lambda i: (i, 0)), const((D_MODEL, PEER_HEADS * PEER_DK)),
                  const(keys.shape), const(keys.shape)],
        out_specs=[tok_spec, tok_spec],
        out_shape=[jax.ShapeDtypeStruct((n, PEER_SLOTS), jnp.int32), jax.ShapeDtypeStruct((n, PEER_SLOTS), jnp.float32)],
        scratch_shapes=[pltpu.VMEM((2 * PEER_TOPK, tb), jnp.float32)] * 2 + [pltpu.VMEM((PEER_TOPK, tb), jnp.float32)] * 2
        + [pltpu.VMEM((PEER_SLOTS, tb), jnp.float32)] * 2,
        compiler_params=pltpu.CompilerParams(dimension_semantics=("arbitrary",)),
        name="peer_route",
    )(ht, lp['peer_wq'].astype(jnp.bfloat16), k_hi, k_lo)


def peer_ffn(h, lp):
    lead = h.shape[:-1]
    ht = h.reshape(-1, D_MODEL)
    e, gate = peer_route(ht, lp)
    out = peer_experts(ht, e, gate, lp['peer_u_packed'], lp['peer_v_packed'])
    return out.reshape(*lead, D_MODEL)


def layer_forward(x, p_l, pos, z_prev0, wkv0, attend, lp):
    h = rmsnorm(x, lp['attn_norm'])
    z = h @ lp['w_in']
    o_r, wkv, shift = rwkv_mix(z[..., :RWKV_COLS], z_prev0, wkv0, lp)
    qn, qr, kvc, kvs, kvw, gates = nsa_project(z[..., RWKV_COLS:RWKV_COLS + NSA_COLS], pos, lp)
    o_n = attend(qn, qr, kvc, kvs, kvw, gates)
    zg = z[..., RWKV_COLS + NSA_COLS:]
    merged = (jax.nn.sigmoid(zg[..., :D_MODEL]) * (o_r @ lp['w_rwkv_out'])
              + jax.nn.sigmoid(zg[..., D_MODEL:]) * (o_n @ lp['w_nsa_out']))
    x = x + merged @ lp['w_out']
    x = x + peer_ffn(rmsnorm(x, lp['ffn_norm']), lp)
    x = x + jax.nn.sigmoid(rmsnorm(x, lp['ple_norm']) @ lp['ple_w_gate']) * (p_l @ lp['ple_w_in'])
    return x, kvc, kvs, kvw, wkv, shift


def kernel(x_prompt, x_sample, cache_cmp_kv, cache_sel_kv, state_win_kv, state_wkv, state_shift, page_table,
           p_prompt, p_sample, attn_norm, w_in, rwkv_mu, rwkv_w0, rwkv_w_up, rwkv_a0, rwkv_a_up, rwkv_g_up,
           rwkv_k_k, rwkv_k_a, rwkv_r_k, rwkv_gn_w, rwkv_gn_b, w_rwkv_out, q_norm, k_norm, cmp_pe, cmp_w1,
           cmp_w2, w_nsa_out, w_out, ffn_norm, peer_wq, peer_keys, peer_u, peer_v, ple_norm, ple_w_gate,
           ple_w_in):
    b_p, s_p = x_prompt.shape[:2]
    pos_p = jnp.arange(s_p)
    pos_s = PAST_LEN + jnp.arange(x_sample.shape[1])
    xp, xs = x_prompt, x_sample
    l = 0
    lp = dict(attn_norm=attn_norm[l], w_in=w_in[l], rwkv_mu=rwkv_mu[l], rwkv_w0=rwkv_w0[l],
              rwkv_w_up=rwkv_w_up[l], rwkv_a0=rwkv_a0[l], rwkv_a_up=rwkv_a_up[l], rwkv_g_up=rwkv_g_up[l],
              rwkv_k_k=rwkv_k_k[l], rwkv_k_a=rwkv_k_a[l], rwkv_r_k=rwkv_r_k[l], rwkv_gn_w=rwkv_gn_w[l],
              rwkv_gn_b=rwkv_gn_b[l], w_rwkv_out=w_rwkv_out[l], q_norm=q_norm[l], k_norm=k_norm[l],
              cmp_pe=cmp_pe[l], cmp_w1=cmp_w1[l], cmp_w2=cmp_w2[l], w_nsa_out=w_nsa_out[l], w_out=w_out[l],
              ffn_norm=ffn_norm[l], peer_wq=peer_wq[l], peer_keys=peer_keys[l], peer_u=peer_u[l],
              peer_v=peer_v[l], ple_norm=ple_norm[l], ple_w_gate=ple_w_gate[l], ple_w_in=ple_w_in[l],
              peer_u_packed=pack_expert_table(peer_u[l]), peer_v_packed=pack_expert_table(peer_v[l]))
    attend_p = functools.partial(nsa_prompt, lp=lp)
    xp, kvc_p, kvs_p, kvw_p, wkv_p, sh_p = layer_forward(
        xp, p_prompt[l], pos_p, jnp.zeros((b_p, RWKV_COLS), xp.dtype),
        jnp.zeros((b_p, RWKV_HEADS, HEAD_DIM, HEAD_DIM), jnp.float32), attend_p, lp)
    win_prev = state_win_kv[l]
    attend_s = functools.partial(nsa_sample, cache_cmp=cache_cmp_kv[l], cache_sel=cache_sel_kv[l],
                                 page_table=page_table, state_win=win_prev, lp=lp)
    xs, kvc_s, kvs_s, kvw_s, wkv_s, sh_s = layer_forward(xs, p_sample[l], pos_s, state_shift[l], state_wkv[l], attend_s, lp)
    win_s = jnp.concatenate([win_prev.astype(kvw_s.dtype), kvw_s], 1)[:, -win_prev.shape[1]:]
    st = lambda a: a[None]
    return (xp, xs, st(kvc_p), st(kvc_s), st(kvs_p), st(kvs_s),
            st(kvw_p[:, -min(WINDOW, s_p):]), st(win_s), st(wkv_p), st(wkv_s), st(sh_p), st(sh_s))
```

```python
import functools
import jax, jax.numpy as jnp
from jax import lax
import numpy as np
from jax.experimental import pallas as pl
from jax.experimental.pallas import tpu as pltpu

LANES = 128
SUBLANES = 8
VMEM_TABLE_LIMIT = 48 * 1024 * 1024

D_MODEL = 1024
BATCH = 8
SEQ = 4096
DEPTH = 1
DEC_BATCH = 128
DEC_SEQ = 8
PAST_LEN = 8192
PAGE_SIZE = 128

HEAD_DIM = 64
RWKV_HEADS = 8
RWKV_WIDTH = RWKV_HEADS * HEAD_DIM
W_RANK = 64
A_RANK = 64
G_RANK = 128
RWKV_COLS = 3 * RWKV_WIDTH + W_RANK + A_RANK + G_RANK
RWKV_SPLITS = (RWKV_WIDTH, 2 * RWKV_WIDTH, 3 * RWKV_WIDTH, 3 * RWKV_WIDTH + W_RANK, 3 * RWKV_WIDTH + W_RANK + A_RANK)
GN_EPS = 64e-5
NSA_HEADS = 8
NSA_KV = 2
NSA_GROUP = NSA_HEADS // NSA_KV
NSA_WIDTH = NSA_HEADS * HEAD_DIM
KV_WIDTH = NSA_KV * HEAD_DIM
N_NSA_BRANCH = 3
NSA_COLS = NSA_WIDTH + N_NSA_BRANCH * 2 * KV_WIDTH + N_NSA_BRANCH * NSA_HEADS
CMP_BLOCK = 32
CMP_STRIDE = 16
CMP_RATIO = CMP_BLOCK // CMP_STRIDE
CMP_HIDDEN = 256
SEL_BLOCK = 64
N_SELECT = 16
WINDOW = 512
Q_BLOCK = 128
ROPE_THETA = 10000.0
IN_COLS = RWKV_COLS + NSA_COLS + 2 * D_MODEL
PEER_HEADS = 8
PEER_DK = 128
N_KEYS = 128
N_EXPERTS = N_KEYS * N_KEYS
PEER_TOPK = 16
PEER_BLOCK = 256
PLE_DIM = 256
RMS_EPS = 1e-6


def rmsnorm(x, g):
    xf = x.astype(jnp.float32)
    y = xf * lax.rsqrt(jnp.mean(xf * xf, -1, keepdims=True) + RMS_EPS)
    return (y * g).astype(x.dtype)


def rope(x, pos):
    half = HEAD_DIM // 2
    inv = ROPE_THETA ** (-jnp.arange(half, dtype=jnp.float32) / half)
    ang = pos.astype(jnp.float32)[:, None] * inv
    cos, sin = jnp.cos(ang)[:, None, :], jnp.sin(ang)[:, None, :]
    xf = x.astype(jnp.float32)
    x1, x2 = xf[..., :half], xf[..., half:]
    return jnp.concatenate([x1 * cos - x2 * sin, x2 * cos + x1 * sin], -1).astype(x.dtype)


def masked_softmax(s, mask):
    s = jnp.where(mask, s.astype(jnp.float32), -jnp.inf)
    m = jnp.max(s, -1, keepdims=True)
    m = jnp.where(jnp.isfinite(m), m, 0.0)
    e = jnp.exp(s - m)
    return e / jnp.maximum(jnp.sum(e, -1, keepdims=True), 1e-30)


WKV_GROUP = SUBLANES
WKV_CHUNK = 32
WKV_COLW = LANES


def _wkv_kernel(kk_ref, kka_ref, d_ref, dr_ref, k_ref, v_ref, c1_ref, c2_ref, s0_ref,
                o_ref, sfin_ref, s_ref, mask_ref, *, chunk):
    f32, bf16 = jnp.float32, jnp.bfloat16
    hd = 64
    c = pl.program_id(1)
    colw = mask_ref.shape[2]
    n_col = s_ref.shape[2] // colw

    @pl.when(c == 0)
    def _():
        s_ref[...] = s0_ref[0]
        lane = lax.broadcasted_iota(jnp.int32, mask_ref.shape, 2)
        row = lax.broadcasted_iota(jnp.int32, mask_ref.shape, 0)
        mask_ref[...] = ((lane % hd) == row).astype(f32)

    ones_bd = (lax.broadcasted_iota(jnp.int32, (colw, colw), 0) // hd
               == lax.broadcasted_iota(jnp.int32, (colw, colw), 1) // hd).astype(bf16)
    rows = hd * WKV_GROUP

    def seg(x3):
        x = x3.reshape(rows, colw)
        hi = x.astype(bf16)
        lo = (x - hi.astype(f32)).astype(bf16)
        y = (jnp.dot(hi, ones_bd, preferred_element_type=f32)
             + jnp.dot(lo, ones_bd, preferred_element_type=f32))
        return y.reshape(hd, WKV_GROUP, colw)

    def step(t, carry):
        kk, kka, d, dr = kk_ref[0, t], kka_ref[0, t], d_ref[0, t], dr_ref[0, t]
        k, v, c1, c2 = k_ref[0, t], v_ref[0, t], c1_ref[0, t], c2_ref[0, t]
        v_hi = v.astype(bf16).astype(f32)
        v_lo = v - v_hi
        for p in range(n_col):
            sl = slice(p * colw, (p + 1) * colw)
            m = mask_ref[...]
            s = s_ref[:, :, sl]
            sa = seg(s * kk[None, :, sl])
            tb = jnp.dot((s * dr[None, :, sl]).reshape(rows, colw).astype(bf16), ones_bd,
                         preferred_element_type=f32).reshape(hd, WKV_GROUP, colw)
            vb = (jnp.dot((m * v_hi[None, :, sl]).reshape(rows, colw).astype(bf16), ones_bd, preferred_element_type=f32)
                  + jnp.dot((m * v_lo[None, :, sl]).reshape(rows, colw).astype(bf16), ones_bd, preferred_element_type=f32)
                  ).reshape(hd, WKV_GROUP, colw)
            s_ref[:, :, sl] = s * d[None, :, sl] - sa * kka[None, :, sl] + vb * k[None, :, sl]
            ob = tb - sa * c1[None, :, sl] + vb * c2[None, :, sl]
            o_ref[0, t, :, sl] = jnp.sum(ob * m, axis=0)
        return carry

    lax.fori_loop(0, chunk, step, 0)

    @pl.when(c == pl.num_programs(1) - 1)
    def _():
        sfin_ref[0] = s_ref[...]


def wkv_scan(r_h, k_h, v_h, decay, kk, a_h, wkv0):
    B, T, H, N = r_h.shape
    width = H * N
    G = B // WKV_GROUP
    chunk = min(WKV_CHUNK, T)
    kka = kk * a_h
    dr = decay * r_h
    c1 = jnp.broadcast_to(jnp.sum(kka * r_h, -1, keepdims=True), r_h.shape)
    c2 = jnp.broadcast_to(jnp.sum(k_h * r_h, -1, keepdims=True), r_h.shape)
    tm = lambda t: t.reshape(G, WKV_GROUP, T, width).transpose(0, 2, 1, 3)
    s0 = wkv0.reshape(G, WKV_GROUP, H, N, N).transpose(0, 3, 1, 2, 4).reshape(G, N, WKV_GROUP, width)
    seq_spec = pl.BlockSpec((1, chunk, WKV_GROUP, width), lambda g, c: (g, c, 0, 0))
    st_spec = pl.BlockSpec((1, N, WKV_GROUP, width), lambda g, c: (g, 0, 0, 0))
    out, s_fin = pl.pallas_call(
        functools.partial(_wkv_kernel, chunk=chunk),
        grid=(G, T // chunk),
        in_specs=[seq_spec] * 8 + [st_spec],
        out_specs=[seq_spec, st_spec],
        out_shape=[jax.ShapeDtypeStruct((G, T, WKV_GROUP, width), jnp.float32),
                   jax.ShapeDtypeStruct((G, N, WKV_GROUP, width), jnp.float32)],
        scratch_shapes=[pltpu.VMEM((N, WKV_GROUP, width), jnp.float32),
                        pltpu.VMEM((N, WKV_GROUP, WKV_COLW), jnp.float32)],
        compiler_params=pltpu.CompilerParams(dimension_semantics=("arbitrary", "arbitrary")),
        name="wkv_scan",
    )(tm(kk), tm(kka), tm(decay), tm(dr), tm(k_h), tm(v_h), tm(c1), tm(c2), s0)
    out = out.transpose(0, 2, 1, 3).reshape(B, T, H, N)
    s_fin = s_fin.reshape(G, N, WKV_GROUP, H, N).transpose(0, 2, 3, 1, 4).reshape(B, H, N, N)
    return out, s_fin


def rwkv_mix(zr, z_prev0, wkv0, lp):
    B, T, _ = zr.shape
    f32 = jnp.float32
    z_prev = jnp.concatenate([z_prev0[:, None, :].astype(zr.dtype), zr[:, :-1]], axis=1)
    xs = zr + (z_prev - zr) * lp['rwkv_mu']
    r, k, v, wl, al, gl = jnp.split(xs, RWKV_SPLITS, axis=-1)
    w = -jax.nn.softplus(-(lp['rwkv_w0'] + jnp.tanh(wl) @ lp['rwkv_w_up'])) - 0.5
    a = jax.nn.sigmoid(lp['rwkv_a0'] + al @ lp['rwkv_a_up'])
    g = jax.nn.sigmoid(gl) @ lp['rwkv_g_up']
    heads = lambda t: t.astype(f32).reshape(B, T, RWKV_HEADS, HEAD_DIM)
    kk = heads(k * lp['rwkv_k_k'])
    kk = kk / jnp.maximum(jnp.sqrt(jnp.sum(kk * kk, -1, keepdims=True)), 1e-12)
    k = k * (1.0 + (a - 1.0) * lp['rwkv_k_a'])
    r_h, k_h, v_h, a_h = heads(r), heads(k), heads(v), heads(a)
    decay = jnp.exp(-jnp.exp(heads(w)))

    out, s_fin = wkv_scan(r_h, k_h, v_h, decay, kk, a_h, wkv0.astype(f32))
    mu = jnp.mean(out, -1, keepdims=True)
    var = jnp.mean(jnp.square(out - mu), -1, keepdims=True)
    on = ((out - mu) * lax.rsqrt(var + GN_EPS)).reshape(B, T, RWKV_WIDTH) * lp['rwkv_gn_w'] + lp['rwkv_gn_b']
    bonus = jnp.sum(r_h * k_h * lp['rwkv_r_k'], -1, keepdims=True) * v_h
    o = (on + bonus.reshape(B, T, RWKV_WIDTH)) * g
    return o.astype(zr.dtype), s_fin, zr[:, -1]


def nsa_project(zn, pos, lp):
    B, T, _ = zn.shape
    q = zn[..., :NSA_WIDTH].reshape(B, T, NSA_HEADS, HEAD_DIM)
    kv = zn[..., NSA_WIDTH:NSA_WIDTH + 6 * KV_WIDTH].reshape(B, T, N_NSA_BRANCH, 2, NSA_KV, HEAD_DIM)
    gates = jax.nn.sigmoid(zn[..., NSA_WIDTH + 6 * KV_WIDTH:].reshape(B, T, NSA_HEADS, N_NSA_BRANCH))
    qn = rmsnorm(q, lp['q_norm'])
    qr = rope(qn, pos)

    def prep(kvb, gk):
        return jnp.stack([rope(rmsnorm(kvb[:, :, 0], gk), pos), kvb[:, :, 1]], axis=2)

    return qn, qr, kv[:, :, 0], prep(kv[:, :, 1], lp['k_norm'][1]), prep(kv[:, :, 2], lp['k_norm'][2]), gates


def compress(kv_raw, lp):
    n_chunk = kv_raw.shape[0] // CMP_STRIDE
    nc = n_chunk - CMP_RATIO + 1
    chunks = kv_raw[:n_chunk * CMP_STRIDE].reshape(n_chunk, CMP_STRIDE, 2, NSA_KV, HEAD_DIM)
    w1 = lp['cmp_w1'].reshape(2, CMP_RATIO, CMP_STRIDE, HEAD_DIM, CMP_HIDDEN)
    part = jnp.einsum('csekd,ersdf->crekf', chunks, w1)
    hid = part[0:nc, 0]
    for r in range(1, CMP_RATIO):
        hid = hid + part[r:r + nc, r]
    pe_bias = jnp.einsum('esd,esdf->ef', lp['cmp_pe'], lp['cmp_w1'].reshape(2, CMP_BLOCK, HEAD_DIM, CMP_HIDDEN))
    hid = jax.nn.gelu(hid + pe_bias[None, :, None, :])
    out = jnp.einsum('nekf,efd->nekd', hid, lp['cmp_w2'])
    kc = rmsnorm(out[:, 0], lp['k_norm'][0])
    return kc, out[:, 1], jnp.arange(nc) * CMP_STRIDE + CMP_BLOCK - 1


def block_cover_matrix(nsb, nc):
    c0 = jnp.arange(nc) * CMP_STRIDE
    s0 = jnp.arange(nsb) * SEL_BLOCK
    cover = (c0[None, :] <= s0[:, None] + SEL_BLOCK - 1) & (c0[None, :] + CMP_BLOCK - 1 >= s0[:, None])
    return cover.astype(jnp.float32)


def nsa_core(qn, qr, q_pos, kc, vc, c_end, nsb, npb, gather_past, kv_new, kv_win, win_pos, gates):
    tq = qn.shape[0]
    scale = HEAD_DIM ** -0.5
    qn_g = qn.reshape(tq, NSA_KV, NSA_GROUP, HEAD_DIM)
    qr_g = qr.reshape(tq, NSA_KV, NSA_GROUP, HEAD_DIM)
    causal = q_pos[:, None] >= q_pos[None, :]
    p_c = masked_softmax(jnp.einsum('tkgd,nkd->kgtn', qn_g, kc) * scale, c_end[None, :] <= q_pos[:, None])
    o_c = jnp.einsum('kgtn,nkd->tkgd', p_c, vc)
    imp = jnp.einsum('kgtn,jn->ktj', p_c, block_cover_matrix(nsb, kc.shape[0]))
    blk = jnp.arange(nsb)[None, :]
    cur = (q_pos // SEL_BLOCK)[:, None]
    forced = (blk == cur) | (blk == cur - 1) | (blk == 0)
    imp = jnp.where(forced, jnp.inf, jnp.where(blk > cur, -jnp.inf, imp))
    n_sel = min(N_SELECT, nsb)
    sel = lax.top_k(imp, n_sel)[1]
    kv_p = gather_past(sel)
    m_len = n_sel * SEL_BLOCK
    k_p = kv_p[..., 0, :].reshape(NSA_KV, tq, m_len, HEAD_DIM)
    v_p = kv_p[..., 1, :].reshape(NSA_KV, tq, m_len, HEAD_DIM)
    s_sel = jnp.concatenate([jnp.einsum('tkgd,ktmd->kgtm', qr_g, k_p),
                             jnp.einsum('tkgd,skd->kgts', qr_g, kv_new[:, 0])], -1) * scale
    m_past = jnp.repeat(sel < npb, SEL_BLOCK, axis=-1)
    member = jnp.any(sel[..., None] == (q_pos // SEL_BLOCK), axis=-2)
    p_s = masked_softmax(s_sel, jnp.concatenate([m_past, member & causal], -1)[:, None])
    o_s = (jnp.einsum('kgtm,ktmd->tkgd', p_s[..., :m_len], v_p)
           + jnp.einsum('kgts,skd->tkgd', p_s[..., m_len:], kv_new[:, 1]))
    dist = q_pos[:, None] - win_pos[None, :]
    m_w = (dist >= 0) & (dist < WINDOW) & (win_pos >= 0)[None, :]
    p_w = masked_softmax(jnp.einsum('tkgd,skd->kgts', qr_g, kv_win[:, 0]) * scale, m_w)
    o_w = jnp.einsum('kgts,skd->tkgd', p_w, kv_win[:, 1])
    g = gates.reshape(tq, NSA_KV, NSA_GROUP, N_NSA_BRANCH)
    o = o_c * g[..., 0:1] + o_s * g[..., 1:2] + o_w * g[..., 2:3]
    return o.reshape(tq, NSA_WIDTH).astype(qn.dtype)


SEL_COLS = 64
SEL_KT = 256
CHUNK_W = CMP_STRIDE * HEAD_DIM
NEG_BIG = -1e30


def _dot_nt(a, b):
    return lax.dot_general(a, b, (((1,), (1,)), ((), ())), preferred_element_type=jnp.float32)


def _split_bf16(x):
    top = lax.bitcast_convert_type(lax.bitcast_convert_type(x, jnp.uint32) & jnp.uint32(0xFFFF0000), jnp.float32)
    return top.astype(jnp.bfloat16), (x - top).astype(jnp.bfloat16)


def _dot3(a, b_hi, b_lo):
    a_hi, a_lo = _split_bf16(a)
    f32 = jnp.float32
    return (jnp.dot(a_hi, b_hi, preferred_element_type=f32) + jnp.dot(a_lo, b_hi, preferred_element_type=f32)
            + jnp.dot(a_hi, b_lo, preferred_element_type=f32))


def _compress_kernel(c_ref, cn_ref, w1h_ref, w1l_ref, pe_ref, w2h_ref, w2l_ref, kn_ref, kc_ref, vc_ref):
    for e in range(2):
        w1h, w1l = w1h_ref[e], w1l_ref[e]
        hid = (_dot3(c_ref[0, e, 0], w1h[:CHUNK_W], w1l[:CHUNK_W])
               + _dot3(cn_ref[0, e, 0], w1h[CHUNK_W:], w1l[CHUNK_W:]))
        pe_bias = _dot3(pe_ref[e], w1h, w1l)[0:1]
        hid = jax.nn.gelu(hid + pe_bias)
        out = _dot3(hid, w2h_ref[e], w2l_ref[e])
        if e == 0:
            out = out * lax.rsqrt(jnp.mean(out * out, -1, keepdims=True) + RMS_EPS) * kn_ref[...]
            kc_ref[0, 0] = out
        else:
            vc_ref[0, 0] = out


def compress_prompt(kv_cmp, lp):
    B, S = kv_cmp.shape[:2]
    nch = S // CMP_STRIDE
    c = kv_cmp.transpose(0, 2, 3, 1, 4).reshape(B, 2, NSA_KV, nch, CHUNK_W)
    cn = jnp.concatenate([c[:, :, :, 1:], jnp.zeros_like(c[:, :, :, :1])], axis=3)
    pe = jnp.broadcast_to(lp['cmp_pe'].reshape(2, 1, CMP_BLOCK * HEAD_DIM), (2, SUBLANES, CMP_BLOCK * HEAD_DIM))
    c_spec = pl.BlockSpec((1, 2, 1, nch, CHUNK_W), lambda b, k: (b, 0, k, 0, 0))
    full = lambda shape: pl.BlockSpec(shape, lambda b, k: (0,) * len(shape))
    o_spec = pl.BlockSpec((1, 1, nch, HEAD_DIM), lambda b, k: (b, k, 0, 0))
    w1_spec = full((2, CMP_BLOCK * HEAD_DIM, CMP_HIDDEN))
    w2_spec = full((2, CMP_HIDDEN, HEAD_DIM))
    return pl.pallas_call(
        _compress_kernel,
        grid=(B, NSA_KV),
        in_specs=[c_spec, c_spec, w1_spec, w1_spec, full((2, SUBLANES, CMP_BLOCK * HEAD_DIM)),
                  w2_spec, w2_spec, full((1, HEAD_DIM))],
        out_specs=[o_spec, o_spec],
        out_shape=[jax.ShapeDtypeStruct((B, NSA_KV, nch, HEAD_DIM), jnp.float32)] * 2,
        compiler_params=pltpu.CompilerParams(dimension_semantics=("arbitrary", "arbitrary")),
        name="nsa_compress",
    )(c, cn, *_split_bf16(lp['cmp_w1']), pe, *_split_bf16(lp['cmp_w2']), lp['k_norm'][0:1])


def _flash_update(q, kt, vt, valid, m, l, acc, scale):
    s = jnp.where(valid, _dot_nt(q, kt) * scale, NEG_BIG)
    m_new = jnp.maximum(m, jnp.max(s, -1, keepdims=True))
    p = jnp.where(valid, jnp.exp(s - m_new), 0.0)
    alpha = jnp.exp(m - m_new)
    l = alpha * l + jnp.sum(p, -1, keepdims=True)
    acc = alpha * acc + jnp.dot(p.astype(jnp.bfloat16), vt, preferred_element_type=jnp.float32)
    return m_new, l, acc


def _nsa_prompt_kernel(qn_ref, qr_ref, kc_ref, vc_ref, ks_ref, vs_ref, kw_ref, vw_ref, g_ref, o_ref):
    f32, bf16 = jnp.float32, jnp.bfloat16
    i = pl.program_id(2)
    t0 = i * Q_BLOCK
    scale = HEAD_DIM ** -0.5
    nch = kc_ref.shape[2]
    t_col = t0 + lax.broadcasted_iota(jnp.int32, (Q_BLOCK, 1), 0)

    kc = kc_ref[0, 0].astype(bf16)
    vc = vc_ref[0, 0].astype(bf16)
    n_row = lax.broadcasted_iota(jnp.int32, (1, nch), 1)
    c_valid = (n_row * CMP_STRIDE + (CMP_BLOCK - 1) <= t_col) & (n_row < nch - 1)
    p_sum = jnp.zeros((Q_BLOCK, nch), f32)
    o_cmp = []
    for g in range(NSA_GROUP):
        s = jnp.where(c_valid, _dot_nt(qn_ref[0, 0, g].astype(bf16), kc) * scale, -jnp.inf)
        m = jnp.max(s, -1, keepdims=True)
        m = jnp.where(m > -jnp.inf, m, 0.0)
        e = jnp.exp(s - m)
        p = e / jnp.maximum(jnp.sum(e, -1, keepdims=True), 1e-30)
        p_sum = p_sum + p
        o_cmp.append(jnp.dot(p.astype(bf16), vc, preferred_element_type=f32))

    n_col = lax.broadcasted_iota(jnp.int32, (nch, SEL_COLS), 0) * CMP_STRIDE
    j_blk = lax.broadcasted_iota(jnp.int32, (nch, SEL_COLS), 1) * SEL_BLOCK
    cover = ((n_col <= j_blk + (SEL_BLOCK - 1)) & (n_col + (CMP_BLOCK - 1) >= j_blk)).astype(bf16)
    p_hi = p_sum.astype(bf16)
    p_lo = (p_sum - p_hi.astype(f32)).astype(bf16)
    imp = jnp.dot(p_hi, cover, preferred_element_type=f32) + jnp.dot(p_lo, cover, preferred_element_type=f32)
    j_row = lax.broadcasted_iota(jnp.int32, (1, SEL_COLS), 1)
    cur = t_col // SEL_BLOCK
    forced = (j_row == cur) | (j_row == cur - 1) | (j_row == 0)
    imp = jnp.where(forced, jnp.inf, jnp.where(j_row > cur, -jnp.inf, imp))
    rank = jnp.zeros((Q_BLOCK, SEL_COLS), f32)
    for jp in range(SEL_COLS):
        col = imp[:, jp:jp + 1]
        rank = rank + jnp.where((col > imp) | ((col == imp) & (jp < j_row)), 1.0, 0.0)
    sel_mask = jnp.where(rank < N_SELECT, 1.0, 0.0).astype(bf16)

    q_rot = [qr_ref[0, 0, g].astype(bf16) for g in range(NSA_GROUP)]

    def init():
        return tuple((jnp.full((Q_BLOCK, 1), NEG_BIG, f32), jnp.zeros((Q_BLOCK, 1), f32),
                      jnp.zeros((Q_BLOCK, HEAD_DIM), f32)) for _ in range(NSA_GROUP))

    def sel_body(c, carry):
        k0 = pl.multiple_of(c * SEL_KT, SEL_KT)
        kt = ks_ref[0, 0, pl.ds(k0, SEL_KT), :].astype(bf16)
        vt = vs_ref[0, 0, pl.ds(k0, SEL_KT), :].astype(bf16)
        kpos = k0 + lax.broadcasted_iota(jnp.int32, (1, SEL_KT), 1)
        expand = (lax.broadcasted_iota(jnp.int32, (SEL_COLS, SEL_KT), 0)
                  == (k0 + lax.broadcasted_iota(jnp.int32, (SEL_COLS, SEL_KT), 1)) // SEL_BLOCK).astype(bf16)
        valid = (jnp.dot(sel_mask, expand, preferred_element_type=f32) > 0.5) & (kpos <= t_col)
        return tuple(_flash_update(q_rot[g], kt, vt, valid, *carry[g], scale) for g in range(NSA_GROUP))

    sel = lax.fori_loop(0, i // (SEL_KT // Q_BLOCK) + 1, sel_body, init())

    def win_body(c, carry):
        k0 = pl.multiple_of(c * SEL_KT, SEL_KT)
        kt = kw_ref[0, 0, pl.ds(k0, SEL_KT), :].astype(bf16)
        vt = vw_ref[0, 0, pl.ds(k0, SEL_KT), :].astype(bf16)
        dist = t_col - (k0 + lax.broadcasted_iota(jnp.int32, (1, SEL_KT), 1))
        valid = (dist >= 0) & (dist < WINDOW)
        return tuple(_flash_update(q_rot[g], kt, vt, valid, *carry[g], scale) for g in range(NSA_GROUP))

    last_tile = i // (SEL_KT // Q_BLOCK)
    win = lax.fori_loop(jnp.maximum(last_tile - WINDOW // SEL_KT, 0), last_tile + 1, win_body, init())

    for g in range(NSA_GROUP):
        gate = g_ref[0, 0, g]
        o_sel = sel[g][2] / jnp.maximum(sel[g][1], 1e-30)
        o_win = win[g][2] / jnp.maximum(win[g][1], 1e-30)
        o_ref[0, 0, g] = o_cmp[g] * gate[:, 0:1] + o_sel * gate[:, 1:2] + o_win * gate[:, 2:3]


def nsa_prompt(qn, qr, kv_cmp, kv_sel, kv_win, gates, lp):
    B, S = qn.shape[:2]
    assert S % SEL_KT == 0 and S <= SEL_COLS * SEL_BLOCK
    nch = S // CMP_STRIDE
    kc, vc = compress_prompt(kv_cmp, lp)
    heads = lambda t: t.reshape(B, S, NSA_KV, NSA_GROUP, t.shape[-1]).transpose(0, 2, 3, 1, 4)
    kv_major = lambda t: t.transpose(0, 2, 1, 3)
    q_spec = pl.BlockSpec((1, 1, NSA_GROUP, Q_BLOCK, HEAD_DIM), lambda b, k, i: (b, k, 0, i, 0))
    g_spec = pl.BlockSpec((1, 1, NSA_GROUP, Q_BLOCK, N_NSA_BRANCH), lambda b, k, i: (b, k, 0, i, 0))
    c_spec = pl.BlockSpec((1, 1, nch, HEAD_DIM), lambda b, k, i: (b, k, 0, 0))
    s_spec = pl.BlockSpec((1, 1, S, HEAD_DIM), lambda b, k, i: (b, k, 0, 0))
    o = pl.pallas_call(
        _nsa_prompt_kernel,
        grid=(B, NSA_KV, S // Q_BLOCK),
        in_specs=[q_spec, q_spec, c_spec, c_spec, s_spec, s_spec, s_spec, s_spec, g_spec],
        out_specs=q_spec,
        out_shape=jax.ShapeDtypeStruct((B, NSA_KV, NSA_GROUP, S, HEAD_DIM), jnp.float32),
        compiler_params=pltpu.CompilerParams(dimension_semantics=("arbitrary", "arbitrary", "arbitrary")),
        name="nsa_prompt",
    )(heads(qn), heads(qr), kc, vc, kv_major(kv_sel[:, :, 0]), kv_major(kv_sel[:, :, 1]),
      kv_major(kv_win[:, :, 0]), kv_major(kv_win[:, :, 1]), heads(gates))
    return o.transpose(0, 3, 1, 2, 4).reshape(B, S, NSA_WIDTH)


PAGE_GROUP = 16
PAGE_CHUNKS = PAGE_SIZE // CMP_STRIDE
KV_ROW = 2 * KV_WIDTH
SAMPLE_ROWS = NSA_HEADS * DEC_SEQ
SEL_COLS_S = 256


def _page_specs(block, n_pages):
    zeros = (0,) * (len(block) - 1)
    return [pl.BlockSpec(block, (lambda b, c, pt, r=r: (pt[b * n_pages + c * PAGE_GROUP + r],) + zeros))
            for r in range(PAGE_GROUP)]


def _compress_sample_kernel(pt_ref, *refs):
    f32, bf16 = jnp.float32, jnp.bfloat16
    pages = refs[:PAGE_GROUP]
    w1h_ref, w1l_ref, pe_ref, w2h_ref, w2l_ref, kn_ref, kc_ref, vc_ref, p0_sc, p1_sc = refs[PAGE_GROUP:]
    c = pl.program_id(1)
    rows = PAGE_GROUP * PAGE_CHUNKS
    r0 = pl.multiple_of(c * rows, rows)
    for e in range(2):
        cm = jnp.concatenate([pg[0, e, k] for k in range(NSA_KV) for pg in pages], axis=0)
        c_hi, c_lo = _split_bf16(cm)
        for part, dst in enumerate((p0_sc, p1_sc)):
            w_hi = w1h_ref[e, part * CHUNK_W:(part + 1) * CHUNK_W, :]
            w_lo = w1l_ref[e, part * CHUNK_W:(part + 1) * CHUNK_W, :]
            y = (jnp.dot(c_hi, w_hi, preferred_element_type=f32) + jnp.dot(c_lo, w_hi, preferred_element_type=f32)
                 + jnp.dot(c_hi, w_lo, preferred_element_type=f32))
            for k in range(NSA_KV):
                dst[e, k, pl.ds(r0, rows), :] = y[k * rows:(k + 1) * rows]

    @pl.when(c == pl.num_programs(1) - 1)
    def _():
        nch = p0_sc.shape[2]
        for e in range(2):
            pe_bias = _dot3(pe_ref[e], w1h_ref[e], w1l_ref[e])[0:1]
            out = jnp.zeros((nch, LANES), f32)
            for k in range(NSA_KV):
                nxt = pltpu.roll(p1_sc[e, k], nch - 1, 0)
                hid = jax.nn.gelu(p0_sc[e, k] + nxt + pe_bias)
                out = out + _dot3(hid, w2h_ref[e, k], w2l_ref[e, k])
            if e == 0:
                lane = lax.broadcasted_iota(jnp.int32, (1, LANES), 1)
                sq = out * out
                ms0 = jnp.sum(jnp.where(lane < HEAD_DIM, sq, 0.0), -1, keepdims=True) / HEAD_DIM
                ms1 = jnp.sum(jnp.where(lane >= HEAD_DIM, sq, 0.0), -1, keepdims=True) / HEAD_DIM
                kc_ref[0] = out * jnp.where(lane < HEAD_DIM, lax.rsqrt(ms0 + RMS_EPS), lax.rsqrt(ms1 + RMS_EPS)) * kn_ref[...]
            else:
                vc_ref[0] = out


def _nsa_sample_kernel(pt_ref, *refs):
    f32, bf16 = jnp.float32, jnp.bfloat16
    pages = refs[:PAGE_GROUP]
    (qn_ref, qr_ref, kc_ref, vc_ref, knew_ref, win_ref, g_ref, o_ref,
     m_sc, l_sc, acc_sc, selm_sc, ocmp_sc) = refs[PAGE_GROUP:]
    c = pl.program_id(1)
    scale = HEAD_DIM ** -0.5
    T = DEC_SEQ
    nch = kc_ref.shape[1]
    keys = PAGE_GROUP * PAGE_SIZE
    t_row = lax.broadcasted_iota(jnp.int32, (SAMPLE_ROWS, 1), 0) % T
    q_rot = qr_ref[0].astype(bf16)

    @pl.when(c == 0)
    def _():
        n_row = lax.broadcasted_iota(jnp.int32, (1, nch), 1)
        s = jnp.where(n_row < nch - 1, _dot_nt(qn_ref[0].astype(bf16), kc_ref[0].astype(bf16)) * scale, -jnp.inf)
        e = jnp.exp(s - jnp.max(s, -1, keepdims=True))
        p = e / jnp.maximum(jnp.sum(e, -1, keepdims=True), 1e-30)
        ocmp_sc[...] = jnp.dot(p.astype(bf16), vc_ref[0].astype(bf16), preferred_element_type=f32)
        grp = NSA_GROUP * T
        p_sum = jnp.concatenate(
            [sum(p[k * grp + g * T:k * grp + (g + 1) * T] for g in range(NSA_GROUP)) for k in range(NSA_KV)], axis=0)
        n_col = lax.broadcasted_iota(jnp.int32, (nch, SEL_COLS_S), 0) * CMP_STRIDE
        j_blk = lax.broadcasted_iota(jnp.int32, (nch, SEL_COLS_S), 1) * SEL_BLOCK
        cover = ((n_col <= j_blk + (SEL_BLOCK - 1)) & (n_col + (CMP_BLOCK - 1) >= j_blk)).astype(bf16)
        p_hi = p_sum.astype(bf16)
        p_lo = (p_sum - p_hi.astype(f32)).astype(bf16)
        imp = jnp.dot(p_hi, cover, preferred_element_type=f32) + jnp.dot(p_lo, cover, preferred_element_type=f32)
        j_row = lax.broadcasted_iota(jnp.int32, (1, SEL_COLS_S), 1)
        cur = (PAST_LEN + lax.broadcasted_iota(jnp.int32, (NSA_KV * T, 1), 0) % T) // SEL_BLOCK
        forced = (j_row == cur) | (j_row == cur - 1) | (j_row == 0)
        imp = jnp.where(forced, jnp.inf, jnp.where(j_row > cur, -jnp.inf, imp))
        rank = jnp.zeros((NSA_KV * T, SEL_COLS_S), f32)
        for jp in range(-(-(PAST_LEN + T) // SEL_BLOCK)):
            col = imp[:, jp:jp + 1]
            rank = rank + jnp.where((col > imp) | ((col == imp) & (jp < j_row)), 1.0, 0.0)
        sel16 = jnp.where(rank < N_SELECT, 1.0, 0.0)
        selm_sc[...] = jnp.concatenate([sel16[k * T:(k + 1) * T] for k in range(NSA_KV) for _ in range(NSA_GROUP)], axis=0)
        m_sc[...] = jnp.full(m_sc.shape, NEG_BIG, f32)
        l_sc[...] = jnp.zeros(l_sc.shape, f32)
        acc_sc[...] = jnp.zeros(acc_sc.shape, f32)

    def update(kt, vt, valid):
        m, l, acc = _flash_update(q_rot, kt, vt, valid, m_sc[:, 0:1], l_sc[:, 0:1], acc_sc[...], scale)
        m_sc[...] = jnp.broadcast_to(m, m_sc.shape)
        l_sc[...] = jnp.broadcast_to(l, l_sc.shape)
        acc_sc[...] = acc

    kt = jnp.concatenate([pg[0, :, 0:KV_WIDTH] for pg in pages], axis=0).astype(bf16)
    vt = jnp.concatenate([pg[0, :, KV_WIDTH:KV_ROW] for pg in pages], axis=0).astype(bf16)
    past_cols = PAST_LEN // SEL_BLOCK
    expand = (lax.broadcasted_iota(jnp.int32, (past_cols, keys), 0)
              == c * (keys // SEL_BLOCK) + lax.broadcasted_iota(jnp.int32, (past_cols, keys), 1) // SEL_BLOCK).astype(bf16)
    update(kt, vt, jnp.dot(selm_sc[:, 0:past_cols].astype(bf16), expand, preferred_element_type=f32) > 0.5)

    @pl.when(c == pl.num_programs(1) - 1)
    def _():
        s_idx = lax.broadcasted_iota(jnp.int32, (1, LANES), 1)
        member = selm_sc[:, past_cols:past_cols + 1] > 0.5
        update(knew_ref[0, :, 0:KV_WIDTH].astype(bf16), knew_ref[0, :, KV_WIDTH:KV_ROW].astype(bf16),
               member & (s_idx <= t_row) & (s_idx < T))
        o_sel = acc_sc[...] / jnp.maximum(l_sc[:, 0:1], 1e-30)
        n_win = win_ref.shape[1]
        win_cache = min(WINDOW, PAST_LEN)
        w_idx = lax.broadcasted_iota(jnp.int32, (1, n_win), 1)
        dist = (PAST_LEN + t_row) - (PAST_LEN - win_cache + w_idx)
        valid = (dist >= 0) & (dist < WINDOW) & (w_idx < win_cache + T)
        m, l, acc = _flash_update(q_rot, win_ref[0, :, 0:KV_WIDTH].astype(bf16), win_ref[0, :, KV_WIDTH:KV_ROW].astype(bf16),
                                  valid, jnp.full((SAMPLE_ROWS, 1), NEG_BIG, f32), jnp.zeros((SAMPLE_ROWS, 1), f32),
                                  jnp.zeros((SAMPLE_ROWS, LANES), f32), scale)
        o_win = acc / jnp.maximum(l, 1e-30)
        o = (ocmp_sc[...] * g_ref[0, :, 0:LANES] + o_sel * g_ref[0, :, LANES:2 * LANES]
             + o_win * g_ref[0, :, 2 * LANES:3 * LANES])
        first = lax.broadcasted_iota(jnp.int32, (SAMPLE_ROWS, 1), 0) < SAMPLE_ROWS // NSA_KV
        o_ref[0] = jnp.where(first, o, pltpu.roll(o, HEAD_DIM, 1))


def nsa_sample(qn, qr, kv_cmp, kv_sel, kv_win, gates, cache_cmp, cache_sel, page_table, state_win, lp):
    B, T = qn.shape[:2]
    n_pages = PAST_LEN // PAGE_SIZE
    n_phys = cache_cmp.shape[0]
    nch = PAST_LEN // CMP_STRIDE
    assert T == DEC_SEQ == SUBLANES and (PAST_LEN + T) // CMP_STRIDE == nch and n_pages % PAGE_GROUP == 0
    assert -(-(PAST_LEN + T) // SEL_BLOCK) <= SEL_COLS_S
    pt = page_table.reshape(B * n_pages)
    n_steps = n_pages // PAGE_GROUP
    const = lambda shape: pl.BlockSpec(shape, lambda b, c, pt: (0,) * len(shape))
    per_seq = lambda shape: pl.BlockSpec((1,) + shape, lambda b, c, pt: (b,) + (0,) * len(shape))
    params = pltpu.CompilerParams(dimension_semantics=("arbitrary", "arbitrary"))

    cmp_pages = cache_cmp.reshape(n_phys, PAGE_CHUNKS, CMP_STRIDE, 2, NSA_KV, HEAD_DIM).transpose(0, 3, 4, 1, 2, 5)
    cmp_pages = cmp_pages.reshape(n_phys, 2, NSA_KV, PAGE_CHUNKS, CHUNK_W)
    pe = jnp.broadcast_to(lp['cmp_pe'].reshape(2, 1, CMP_BLOCK * HEAD_DIM), (2, SUBLANES, CMP_BLOCK * HEAD_DIM))
    w2 = lp['cmp_w2']
    zero = jnp.zeros_like(w2)
    w2_half = jnp.stack([jnp.concatenate([w2, zero], -1), jnp.concatenate([zero, w2], -1)], axis=1)
    kn = jnp.tile(lp['k_norm'][0:1], (1, NSA_KV))
    kc, vc = pl.pallas_call(
        _compress_sample_kernel,
        grid_spec=pltpu.PrefetchScalarGridSpec(
            num_scalar_prefetch=1, grid=(B, n_steps),
            in_specs=_page_specs((1, 2, NSA_KV, PAGE_CHUNKS, CHUNK_W), n_pages)
            + [const((2, CMP_BLOCK * HEAD_DIM, CMP_HIDDEN))] * 2 + [const((2, SUBLANES, CMP_BLOCK * HEAD_DIM))]
            + [const((2, NSA_KV, CMP_HIDDEN, LANES))] * 2 + [const((1, LANES))],
            out_specs=[per_seq((nch, LANES)), per_seq((nch, LANES))],
            scratch_shapes=[pltpu.VMEM((2, NSA_KV, nch, CMP_HIDDEN), jnp.float32)] * 2),
        out_shape=[jax.ShapeDtypeStruct((B, nch, LANES), jnp.float32)] * 2,
        compiler_params=params,
        name="nsa_compress_sample",
    )(pt, *([cmp_pages] * PAGE_GROUP), *_split_bf16(lp['cmp_w1']), pe, *_split_bf16(w2_half), kn)

    def rows(t):
        return t.reshape(B, T, NSA_KV, NSA_GROUP, t.shape[-1]).transpose(0, 2, 3, 1, 4).reshape(B, SAMPLE_ROWS, t.shape[-1])

    def lane_halves(q):
        first = (jnp.arange(SAMPLE_ROWS) < SAMPLE_ROWS // NSA_KV)[None, :, None]
        return jnp.concatenate([jnp.where(first, q, 0.0), jnp.where(first, 0.0, q)], -1)

    flat = lambda t: t.reshape(t.shape[0], t.shape[1], KV_ROW)
    pad_rows = lambda t, n: jnp.pad(t, ((0, 0), (0, n - t.shape[1]), (0, 0)))
    k_new = pad_rows(flat(kv_sel), LANES)
    win_cache = state_win.shape[1]
    n_win = -(-(win_cache + T) // LANES) * LANES
    win_all = pad_rows(jnp.concatenate([flat(state_win), flat(kv_win)], 1), n_win)
    sel_pages = cache_sel.reshape(n_phys, PAGE_SIZE, KV_ROW)
    o = pl.pallas_call(
        _nsa_sample_kernel,
        grid_spec=pltpu.PrefetchScalarGridSpec(
            num_scalar_prefetch=1, grid=(B, n_steps),
            in_specs=_page_specs((1, PAGE_SIZE, KV_ROW), n_pages)
            + [per_seq((SAMPLE_ROWS, LANES)), per_seq((SAMPLE_ROWS, LANES)), per_seq((nch, LANES)), per_seq((nch, LANES)),
               per_seq((LANES, KV_ROW)), per_seq((n_win, KV_ROW)), per_seq((SAMPLE_ROWS, N_NSA_BRANCH * LANES))],
            out_specs=per_seq((SAMPLE_ROWS, LANES)),
            scratch_shapes=[pltpu.VMEM((SAMPLE_ROWS, LANES), jnp.float32)] * 3
            + [pltpu.VMEM((SAMPLE_ROWS, SEL_COLS_S), jnp.float32), pltpu.VMEM((SAMPLE_ROWS, LANES), jnp.float32)]),
        out_shape=jax.ShapeDtypeStruct((B, SAMPLE_ROWS, LANES), jnp.float32),
        compiler_params=params,
        name="nsa_sample",
    )(pt, *([sel_pages] * PAGE_GROUP), lane_halves(rows(qn)), lane_halves(rows(qr)), kc, vc, k_new, win_all,
      jnp.repeat(rows(gates), LANES, axis=-1))
    o = o[:, :, :HEAD_DIM]
    return o.reshape(B, NSA_KV, NSA_GROUP, T, HEAD_DIM).transpose(0, 3, 1, 2, 4).reshape(B, T, NSA_WIDTH)


PEER_SLOTS = PEER_HEADS * PEER_TOPK
PEER_TOK = 128
HALF_D = D_MODEL // 2
ROW_SUB = HALF_D // LANES


def pack_expert_table(tab):
    bits = lax.bitcast_convert_type(tab.astype(jnp.bfloat16), jnp.uint16).astype(jnp.uint32)
    return ((bits[:, HALF_D:] << 16) | bits[:, :HALF_D]).reshape(tab.shape[0], ROW_SUB, LANES)


def _unpack_row(row):
    lo = lax.bitcast_convert_type(row << 16, jnp.float32)
    hi = lax.bitcast_convert_type(row & jnp.uint32(0xFFFF0000), jnp.float32)
    return lo, hi


def _load_table_and_indices(idx_hbm, tab_hbm, tab_vmem, idx_smem, sem):
    i = pl.program_id(0)

    @pl.when(i == 0)
    def _():
        cp = pltpu.make_async_copy(tab_hbm, tab_vmem, sem.at[0])
        cp.start()
        cp.wait()

    n = idx_smem.shape[0]
    cp = pltpu.make_async_copy(idx_hbm.at[pl.ds(i * n, n)], idx_smem, sem.at[1])
    cp.start()
    cp.wait()


def _eye():
    return (lax.broadcasted_iota(jnp.int32, (PEER_SLOTS, LANES), 0)
            == lax.broadcasted_iota(jnp.int32, (PEER_SLOTS, LANES), 1))


def _peer_act_kernel(idx_hbm, h_ref, gate_ref, tab_hbm, w_ref, tab_vmem, idx_smem, ps_ref, a_ref, sem):
    _load_table_and_indices(idx_hbm, tab_hbm, tab_vmem, idx_smem, sem)
    eye = _eye()

    def tok(t, carry):
        h_lo = h_ref[t, 0:ROW_SUB, :]
        h_hi = h_ref[t, ROW_SUB:2 * ROW_SUB, :]
        for j in range(PEER_SLOTS):
            lo, hi = _unpack_row(tab_vmem[idx_smem[t * PEER_SLOTS + j]])
            ps_ref[j * ROW_SUB:(j + 1) * ROW_SUB, :] = lo * h_lo + hi * h_hi
        q = ps_ref[pl.ds(0, PEER_SLOTS, stride=ROW_SUB), :]
        for s in range(1, ROW_SUB):
            q = q + ps_ref[pl.ds(s, PEER_SLOTS, stride=ROW_SUB), :]
        a_col = jnp.sum(q, axis=-1, keepdims=True)
        a_ref[pl.ds(t, 1), :] = jnp.sum(jnp.where(eye, a_col, 0.0), axis=0, keepdims=True)
        return carry

    lax.fori_loop(0, h_ref.shape[0], tok, 0)
    w_ref[...] = gate_ref[...] * jax.nn.gelu(a_ref[...])


def _peer_out_kernel(idx_hbm, w_ref, tab_hbm, o_ref, tab_vmem, idx_smem, wb_ref, sem):
    _load_table_and_indices(idx_hbm, tab_hbm, tab_vmem, idx_smem, sem)
    eye = _eye()
    n_acc = 4

    def tok(t, carry):
        w_row = w_ref[pl.ds(t, 1), :]
        w_col = jnp.sum(jnp.where(eye, w_row, 0.0), axis=-1, keepdims=True)
        wb_ref[...] = jnp.broadcast_to(w_col, (PEER_SLOTS, LANES))
        acc_lo = [jnp.zeros((ROW_SUB, LANES), jnp.float32) for _ in range(n_acc)]
        acc_hi = [jnp.zeros((ROW_SUB, LANES), jnp.float32) for _ in range(n_acc)]
        for j in range(PEER_SLOTS):
            lo, hi = _unpack_row(tab_vmem[idx_smem[t * PEER_SLOTS + j]])
            wj = wb_ref[j:j + 1, :]
            acc_lo[j % n_acc] = acc_lo[j % n_acc] + wj * lo
            acc_hi[j % n_acc] = acc_hi[j % n_acc] + wj * hi
        o_ref[t, 0:ROW_SUB, :] = (acc_lo[0] + acc_lo[1]) + (acc_lo[2] + acc_lo[3])
        o_ref[t, ROW_SUB:2 * ROW_SUB, :] = (acc_hi[0] + acc_hi[1]) + (acc_hi[2] + acc_hi[3])
        return carry

    lax.fori_loop(0, w_ref.shape[0], tok, 0)


def peer_experts(ht, e, gate, u_packed, v_packed):
    n = ht.shape[0]
    n_exp = u_packed.shape[0]
    tb = min(PEER_TOK, n)
    assert n % tb == 0 and tb % SUBLANES == 0
    idx =e.reshape(n * PEER_SLOTS)
    h3 = ht.reshape(n, 2 * ROW_SUB, LANES)
    any_spec = pl.BlockSpec(memory_space=pl.ANY)
    tok_spec = pl.BlockSpec((tb, PEER_SLOTS), lambda i: (i, 0))
    row_spec = pl.BlockSpec((tb, 2 * ROW_SUB, LANES), lambda i: (i, 0, 0))
    params = pltpu.CompilerParams(dimension_semantics=("arbitrary",), vmem_limit_bytes=VMEM_TABLE_LIMIT)
    table = pltpu.VMEM((n_exp, ROW_SUB, LANES), jnp.uint32)
    idx_smem = pltpu.SMEM((tb * PEER_SLOTS,), jnp.int32)
    w = pl.pallas_call(
        _peer_act_kernel,
        grid=(n // tb,),
        in_specs=[any_spec, row_spec, tok_spec, any_spec],
        out_specs=tok_spec,
        out_shape=jax.ShapeDtypeStruct((n, PEER_SLOTS), jnp.float32),
        scratch_shapes=[table, idx_smem, pltpu.VMEM((PEER_SLOTS * ROW_SUB, LANES), jnp.float32),
                        pltpu.VMEM((tb, PEER_SLOTS), jnp.float32), pltpu.SemaphoreType.DMA((2,))],
        compiler_params=params,
        name="peer_act",
    )(idx, h3, gate, u_packed)
    out = pl.pallas_call(
        _peer_out_kernel,
        grid=(n // tb,),
        in_specs=[any_spec, tok_spec, any_spec],
        out_specs=row_spec,
        out_shape=jax.ShapeDtypeStruct((n, 2 * ROW_SUB, LANES), jnp.float32),
        scratch_shapes=[table, idx_smem, pltpu.VMEM((PEER_SLOTS, LANES), jnp.float32),
                        pltpu.SemaphoreType.DMA((2,))],
        compiler_params=params,
        name="peer_out",
    )(idx, w, v_packed)
    return out.reshape(n, D_MODEL)


def _top_rows(s, row, n_top, val_sc, idx_sc, base):
    for r in range(n_top):
        m = jnp.max(s, axis=0, keepdims=True)
        idx = jnp.min(jnp.where(s == m, row, float(1 << 20)), axis=0, keepdims=True)
        s = jnp.where(row == idx, -jnp.inf, s)
        val_sc[pl.ds(base + r, 1), :] = m
        idx_sc[pl.ds(base + r, 1), :] = idx


def _peer_route_kernel(h_ref, wq_ref, khi_ref, klo_ref, e_ref, g_ref, sv_sc, si_sc, cv_sc, ci_sc, eo_sc, go_sc):
    f32, bf16 = jnp.float32, jnp.bfloat16
    tb = h_ref.shape[0]
    k = PEER_TOPK
    h_bf = h_ref[...].astype(bf16)
    key_row = lax.broadcasted_iota(jnp.int32, (N_KEYS, tb), 0).astype(f32)
    n_pair = k + (k - 1) * (k // 2)
    rr = lax.broadcasted_iota(jnp.int32, (n_pair, tb), 0)
    pair_a = jnp.where(rr < k, 0, 1 + (rr - k) // (k // 2))
    pair_b = jnp.where(rr < k, rr, (rr - k) % (k // 2))
    pair_ok = (pair_a + 1) * (pair_b + 1) <= k
    pair_row = (pair_a * k + pair_b).astype(f32)

    def head(hh, carry):
        col = pl.multiple_of(hh * PEER_DK, PEER_DK)
        q = jnp.dot(h_bf, wq_ref[:, pl.ds(col, PEER_DK)], preferred_element_type=f32)
        q_hi, q_lo = _split_bf16(q)
        for c in range(2):
            k_hi, k_lo = khi_ref[hh, c], klo_ref[hh, c]
            s = _dot_nt(k_hi, q_hi) + _dot_nt(k_lo, q_hi) + _dot_nt(k_hi, q_lo)
            _top_rows(s, key_row, k, sv_sc, si_sc, c * k)
        sv0, sv1 = sv_sc[0:k, :], sv_sc[k:2 * k, :]
        si0, si1 = si_sc[0:k, :], si_sc[k:2 * k, :]
        cand = jnp.concatenate([sv0[0:1, :] + sv1] + [sv0[a:a + 1, :] + sv1[0:k // 2, :] for a in range(1, k)], axis=0)
        cand = jnp.where(pair_ok, cand, -jnp.inf)
        ids = jnp.concatenate([si0[0:1, :] * float(N_KEYS) + si1]
                              + [si0[a:a + 1, :] * float(N_KEYS) + si1[0:k // 2, :] for a in range(1, k)], axis=0)
        _top_rows(cand, pair_row, k, cv_sc, ci_sc, 0)
        ci = ci_sc[...]
        base = pl.multiple_of(hh * k, k)
        for r in range(k):
            eo_sc[pl.ds(base + r, 1), :] = jnp.sum(jnp.where(pair_row == ci[r:r + 1, :], ids, 0.0), axis=0, keepdims=True)
        ex = jnp.exp(cv_sc[...] - cv_sc[0:1, :])
        go_sc[pl.ds(base, k), :] = ex / jnp.sum(ex, axis=0, keepdims=True)
        return carry

    lax.fori_loop(0, PEER_HEADS, head, 0)
    e_ref[...] = eo_sc[...].T.astype(jnp.int32)
    g_ref[...] = go_sc[...].T


def peer_route(ht, lp):
    n = ht.shape[0]
    tb = min(PEER_TOK, n)
    assert n % tb == 0 and tb == PEER_SLOTS
    half = PEER_DK // 2
    keys = lp['peer_keys']
    zero = jnp.zeros_like(keys[:, 0])
    keys = jnp.stack([jnp.concatenate([keys[:, 0], zero], -1), jnp.concatenate([zero, keys[:, 1]], -1)], axis=1)
    k_hi, k_lo = _split_bf16(keys)
    const = lambda shape: pl.BlockSpec(shape, lambda i: (0,) * len(shape))
    tok_spec = pl.BlockSpec((tb, PEER_SLOTS), lambda i: (i, 0))
    return pl.pallas_call(
        _peer_route_kernel,
        grid=(n // tb,),
        in_specs=[pl.BlockSpec((tb, D_MODEL), lambda i: (i, 0)), const((D_MODEL, PEER_HEADS * PEER_DK)),
                  const(keys.shape), const(keys.shape)],
        out_specs=[tok_spec, tok_spec],
        out_shape=[jax.ShapeDtypeStruct((n, PEER_SLOTS), jnp.int32), jax.ShapeDtypeStruct((n, PEER_SLOTS), jnp.float32)],
        scratch_shapes=[pltpu.VMEM((2 * PEER_TOPK, tb), jnp.float32)] * 2 + [pltpu.VMEM((PEER_TOPK, tb), jnp.float32)] * 2
        + [pltpu.VMEM((PEER_SLOTS, tb), jnp.float32)] * 2,
        compiler_params=pltpu.CompilerParams(dimension_semantics=("arbitrary",)),
        name="peer_route",
    )(ht, lp['peer_wq'].astype(jnp.bfloat16), k_hi, k_lo)


def peer_ffn(h, lp):
    lead = h.shape[:-1]
    ht = h.reshape(-1, D_MODEL)
    e, gate = peer_route(ht, lp)
    out = peer_experts(ht, e, gate, lp['peer_u_packed'], lp['peer_v_packed'])
    return out.reshape(*lead, D_MODEL)


MATMUL_OUT_BYTES = 6 * 1024 * 1024
MATMUL_VMEM_LIMIT = 48 * 1024 * 1024


def _matmul_kernel(x_ref, w_ref, o_ref):
    o_ref[...] = jnp.dot(x_ref[...].astype(jnp.bfloat16), w_ref[...], preferred_element_type=jnp.float32)


def _norm_matmul_kernel(x_ref, g_ref, w_ref, o_ref):
    x = x_ref[...]
    h = x * lax.rsqrt(jnp.mean(x * x, -1, keepdims=True) + RMS_EPS) * g_ref[...]
    o_ref[...] = jnp.dot(h.astype(jnp.bfloat16), w_ref[...], preferred_element_type=jnp.float32)


def dense(x, w, gain=None):
    lead, k = x.shape[:-1], x.shape[-1]
    n = w.shape[1]
    n_pad = -(-n // LANES) * LANES
    x2 = x.reshape(-1, k)
    m = x2.shape[0]
    tm = m
    while tm * n_pad * 4 > MATMUL_OUT_BYTES and tm % (2 * SUBLANES) == 0:
        tm //= 2
    w_bf = jnp.pad(w, ((0, 0), (0, n_pad - n))).astype(jnp.bfloat16)
    x_spec = pl.BlockSpec((tm, k), lambda i: (i, 0))
    w_spec = pl.BlockSpec((k, n_pad), lambda i: (0, 0))
    o_spec = pl.BlockSpec((tm, n_pad), lambda i: (i, 0))
    params = pltpu.CompilerParams(dimension_semantics=("arbitrary",), vmem_limit_bytes=MATMUL_VMEM_LIMIT)
    out_shape = jax.ShapeDtypeStruct((m, n_pad), jnp.float32)
    if gain is None:
        out = pl.pallas_call(_matmul_kernel, grid=(m // tm,), in_specs=[x_spec, w_spec], out_specs=o_spec,
                             out_shape=out_shape, compiler_params=params, name="dense")(x2, w_bf)
    else:
        g_spec = pl.BlockSpec((1, k), lambda i: (0, 0))
        out = pl.pallas_call(_norm_matmul_kernel, grid=(m // tm,), in_specs=[x_spec, g_spec, w_spec], out_specs=o_spec,
                             out_shape=out_shape, compiler_params=params, name="norm_dense")(x2, gain.reshape(1, k), w_bf)
    return out[:, :n].reshape(*lead, n)


def layer_forward(x, p_l, pos, z_prev0, wkv0, attend, lp):
    z = dense(x, lp['w_in'], gain=lp['attn_norm'])
    o_r, wkv, shift = rwkv_mix(z[..., :RWKV_COLS], z_prev0, wkv0, lp)
    qn, qr, kvc, kvs, kvw, gates = nsa_project(z[..., RWKV_COLS:RWKV_COLS + NSA_COLS], pos, lp)
    o_n = attend(qn, qr, kvc, kvs, kvw, gates)
    zg = z[..., RWKV_COLS + NSA_COLS:]
    merged = (jax.nn.sigmoid(zg[..., :D_MODEL]) * dense(o_r, lp['w_rwkv_out'])
              + jax.nn.sigmoid(zg[..., D_MODEL:]) * dense(o_n, lp['w_nsa_out']))
    x = x + dense(merged, lp['w_out'])
    x = x + peer_ffn(rmsnorm(x, lp['ffn_norm']), lp)
    x = x + jax.nn.sigmoid(dense(x, lp['ple_w_gate'], gain=lp['ple_norm'])) * dense(p_l, lp['ple_w_in'])
    return x, kvc, kvs, kvw, wkv, shift


def kernel(x_prompt, x_sample, cache_cmp_kv, cache_sel_kv, state_win_kv, state_wkv, state_shift, page_table,
           p_prompt, p_sample, attn_norm, w_in, rwkv_mu, rwkv_w0, rwkv_w_up, rwkv_a0, rwkv_a_up, rwkv_g_up,
           rwkv_k_k, rwkv_k_a, rwkv_r_k, rwkv_gn_w, rwkv_gn_b, w_rwkv_out, q_norm, k_norm, cmp_pe, cmp_w1,
           cmp_w2, w_nsa_out, w_out, ffn_norm, peer_wq, peer_keys, peer_u, peer_v, ple_norm, ple_w_gate,
           ple_w_in):
    b_p, s_p = x_prompt.shape[:2]
    pos_p = jnp.arange(s_p)
    pos_s = PAST_LEN + jnp.arange(x_sample.shape[1])
    xp, xs = x_prompt, x_sample
    l = 0
    lp = dict(attn_norm=attn_norm[l], w_in=w_in[l], rwkv_mu=rwkv_mu[l], rwkv_w0=rwkv_w0[l],
              rwkv_w_up=rwkv_w_up[l], rwkv_a0=rwkv_a0[l], rwkv_a_up=rwkv_a_up[l], rwkv_g_up=rwkv_g_up[l],
              rwkv_k_k=rwkv_k_k[l], rwkv_k_a=rwkv_k_a[l], rwkv_r_k=rwkv_r_k[l], rwkv_gn_w=rwkv_gn_w[l],
              rwkv_gn_b=rwkv_gn_b[l], w_rwkv_out=w_rwkv_out[l], q_norm=q_norm[l], k_norm=k_norm[l],
              cmp_pe=cmp_pe[l], cmp_w1=cmp_w1[l], cmp_w2=cmp_w2[l], w_nsa_out=w_nsa_out[l], w_out=w_out[l],
              ffn_norm=ffn_norm[l], peer_wq=peer_wq[l], peer_keys=peer_keys[l], peer_u=peer_u[l],
              peer_v=peer_v[l], ple_norm=ple_norm[l], ple_w_gate=ple_w_gate[l], ple_w_in=ple_w_in[l],
              peer_u_packed=pack_expert_table(peer_u[l]), peer_v_packed=pack_expert_table(peer_v[l]))
    attend_p = functools.partial(nsa_prompt, lp=lp)
    xp, kvc_p, kvs_p, kvw_p, wkv_p, sh_p = layer_forward(
        xp, p_prompt[l], pos_p, jnp.zeros((b_p, RWKV_COLS), xp.dtype),
        jnp.zeros((b_p, RWKV_HEADS, HEAD_DIM, HEAD_DIM), jnp.float32), attend_p, lp)
    win_prev = state_win_kv[l]
    attend_s = functools.partial(nsa_sample, cache_cmp=cache_cmp_kv[l], cache_sel=cache_sel_kv[l],
                                 page_table=page_table, state_win=win_prev, lp=lp)
    xs, kvc_s, kvs_s, kvw_s, wkv_s, sh_s = layer_forward(xs, p_sample[l], pos_s, state_shift[l], state_wkv[l], attend_s, lp)
    win_s = jnp.concatenate([win_prev.astype(kvw_s.dtype), kvw_s], 1)[:, -win_prev.shape[1]:]
    st = lambda a: a[None]
    return (xp, xs, st(kvc_p), st(kvc_s), st(kvs_p), st(kvs_s),
            st(kvw_p[:, -min(WINDOW, s_p):]), st(win_s), st(wkv_p), st(wkv_s), st(sh_p), st(sh_s))
```

```python
import functools
import jax, jax.numpy as jnp
from jax import lax
from jax.experimental import pallas as pl
from jax.experimental.pallas import tpu as pltpu

LANES = 128
SUBLANES = 8
VMEM_TABLE_LIMIT = 48 * 1024 * 1024

D_MODEL = 1024
BATCH = 8
SEQ = 4096
DEPTH = 1
DEC_BATCH = 128
DEC_SEQ = 8
PAST_LEN = 8192
PAGE_SIZE = 128

HEAD_DIM = 64
RWKV_HEADS = 8
RWKV_WIDTH = RWKV_HEADS * HEAD_DIM
W_RANK = 64
A_RANK = 64
G_RANK = 128
RWKV_COLS = 3 * RWKV_WIDTH + W_RANK + A_RANK + G_RANK
RWKV_SPLITS = (RWKV_WIDTH, 2 * RWKV_WIDTH, 3 * RWKV_WIDTH, 3 * RWKV_WIDTH + W_RANK, 3 * RWKV_WIDTH + W_RANK + A_RANK)
GN_EPS = 64e-5
NSA_HEADS = 8
NSA_KV = 2
NSA_GROUP = NSA_HEADS // NSA_KV
NSA_WIDTH = NSA_HEADS * HEAD_DIM
KV_WIDTH = NSA_KV * HEAD_DIM
N_NSA_BRANCH = 3
NSA_COLS = NSA_WIDTH + N_NSA_BRANCH * 2 * KV_WIDTH + N_NSA_BRANCH * NSA_HEADS
CMP_BLOCK = 32
CMP_STRIDE = 16
CMP_RATIO = CMP_BLOCK // CMP_STRIDE
CMP_HIDDEN = 256
SEL_BLOCK = 64
N_SELECT = 16
WINDOW = 512
Q_BLOCK = 128
ROPE_THETA = 10000.0
IN_COLS = RWKV_COLS + NSA_COLS + 2 * D_MODEL
PEER_HEADS = 8
PEER_DK = 128
N_KEYS = 128
N_EXPERTS = N_KEYS * N_KEYS
PEER_TOPK = 16
PEER_BLOCK = 256
PLE_DIM = 256
RMS_EPS = 1e-6


def rmsnorm(x, g):
    xf = x.astype(jnp.float32)
    y = xf * lax.rsqrt(jnp.mean(xf * xf, -1, keepdims=True) + RMS_EPS)
    return (y * g).astype(x.dtype)


def rope(x, pos):
    half = HEAD_DIM // 2
    inv = ROPE_THETA ** (-jnp.arange(half, dtype=jnp.float32) / half)
    ang = pos.astype(jnp.float32)[:, None] * inv
    cos, sin = jnp.cos(ang)[:, None, :], jnp.sin(ang)[:, None, :]
    xf = x.astype(jnp.float32)
    x1, x2 = xf[..., :half], xf[..., half:]
    return jnp.concatenate([x1 * cos - x2 * sin, x2 * cos + x1 * sin], -1).astype(x.dtype)


WKV_GROUP = SUBLANES
WKV_CHUNK = 32
WKV_COLW = LANES


def _wkv_kernel(kk_ref, kka_ref, d_ref, dr_ref, k_ref, v_ref, c1_ref, c2_ref, s0_ref,
                o_ref, sfin_ref, s_ref, mask_ref, *, chunk):
    f32, bf16 = jnp.float32, jnp.bfloat16
    hd = 64
    c = pl.program_id(1)
    colw = mask_ref.shape[2]
    n_col = s_ref.shape[2] // colw

    @pl.when(c == 0)
    def _():
        s_ref[...] = s0_ref[0]
        lane = lax.broadcasted_iota(jnp.int32, mask_ref.shape, 2)
        row = lax.broadcasted_iota(jnp.int32, mask_ref.shape, 0)
        mask_ref[...] = ((lane % hd) == row).astype(f32)

    ones_bd = (lax.broadcasted_iota(jnp.int32, (colw, colw), 0) // hd
               == lax.broadcasted_iota(jnp.int32, (colw, colw), 1) // hd).astype(bf16)
    rows = hd * WKV_GROUP

    def seg(x3):
        x = x3.reshape(rows, colw)
        hi = x.astype(bf16)
        lo = (x - hi.astype(f32)).astype(bf16)
        y = (jnp.dot(hi, ones_bd, preferred_element_type=f32)
             + jnp.dot(lo, ones_bd, preferred_element_type=f32))
        return y.reshape(hd, WKV_GROUP, colw)

    def step(t, carry):
        kk, kka, d, dr = kk_ref[0, t], kka_ref[0, t], d_ref[0, t], dr_ref[0, t]
        k, v, c1, c2 = k_ref[0, t], v_ref[0, t], c1_ref[0, t], c2_ref[0, t]
        for p in range(n_col):
            sl = slice(p * colw, (p + 1) * colw)
            m = mask_ref[...]
            s = s_ref[:, :, sl]
            sa = seg(s * kk[None, :, sl])
            tb = jnp.dot((s * dr[None, :, sl]).reshape(rows, colw).astype(bf16), ones_bd,
                         preferred_element_type=f32).reshape(hd, WKV_GROUP, colw)
            vb = jnp.dot((m * v[None, :, sl]).reshape(rows, colw).astype(bf16), ones_bd,
                         preferred_element_type=f32).reshape(hd, WKV_GROUP, colw)
            s_ref[:, :, sl] = s * d[None, :, sl] - sa * kka[None, :, sl] + vb * k[None, :, sl]
            ob = tb - sa * c1[None, :, sl] + vb * c2[None, :, sl]
            o_ref[0, t, :, sl] = jnp.sum(ob * m, axis=0)
        return carry

    lax.fori_loop(0, chunk, step, 0)

    @pl.when(c == pl.num_programs(1) - 1)
    def _():
        sfin_ref[0] = s_ref[...]


def wkv_scan(r_h, k_h, v_h, decay, kk, a_h, wkv0):
    B, T, H, N = r_h.shape
    width = H * N
    G = B // WKV_GROUP
    chunk = min(WKV_CHUNK, T)
    kka = kk * a_h
    dr = decay * r_h
    c1 = jnp.broadcast_to(jnp.sum(kka * r_h, -1, keepdims=True), r_h.shape)
    c2 = jnp.broadcast_to(jnp.sum(k_h * r_h, -1, keepdims=True), r_h.shape)
    tm = lambda t: t.reshape(G, WKV_GROUP, T, width).transpose(0, 2, 1, 3)
    s0 = wkv0.reshape(G, WKV_GROUP, H, N, N).transpose(0, 3, 1, 2, 4).reshape(G, N, WKV_GROUP, width)
    seq_spec = pl.BlockSpec((1, chunk, WKV_GROUP, width), lambda g, c: (g, c, 0, 0))
    st_spec = pl.BlockSpec((1, N, WKV_GROUP, width), lambda g, c: (g, 0, 0, 0))
    out, s_fin = pl.pallas_call(
        functools.partial(_wkv_kernel, chunk=chunk),
        grid=(G, T // chunk),
        in_specs=[seq_spec] * 8 + [st_spec],
        out_specs=[seq_spec, st_spec],
        out_shape=[jax.ShapeDtypeStruct((G, T, WKV_GROUP, width), jnp.float32),
                   jax.ShapeDtypeStruct((G, N, WKV_GROUP, width), jnp.float32)],
        scratch_shapes=[pltpu.VMEM((N, WKV_GROUP, width), jnp.float32),
                        pltpu.VMEM((N, WKV_GROUP, WKV_COLW), jnp.float32)],
        compiler_params=pltpu.CompilerParams(dimension_semantics=("arbitrary", "arbitrary")),
        name="wkv_scan",
    )(tm(kk), tm(kka), tm(decay), tm(dr), tm(k_h), tm(v_h), tm(c1), tm(c2), s0)
    out = out.transpose(0, 2, 1, 3).reshape(B, T, H, N)
    s_fin = s_fin.reshape(G, N, WKV_GROUP, H, N).transpose(0, 2, 3, 1, 4).reshape(B, H, N, N)
    return out, s_fin


def rwkv_mix(zr, z_prev0, wkv0, lp):
    B, T, _ = zr.shape
    f32 = jnp.float32
    z_prev = jnp.concatenate([z_prev0[:, None, :].astype(zr.dtype), zr[:, :-1]], axis=1)
    xs = zr + (z_prev - zr) * lp['rwkv_mu']
    r, k, v, wl, al, gl = jnp.split(xs, RWKV_SPLITS, axis=-1)
    w = -jax.nn.softplus(-(lp['rwkv_w0'] + jnp.tanh(wl) @ lp['rwkv_w_up'])) - 0.5
    a = jax.nn.sigmoid(lp['rwkv_a0'] + al @ lp['rwkv_a_up'])
    g = jax.nn.sigmoid(gl) @ lp['rwkv_g_up']
    heads = lambda t: t.astype(f32).reshape(B, T, RWKV_HEADS, HEAD_DIM)
    kk = heads(k * lp['rwkv_k_k'])
    kk = kk / jnp.maximum(jnp.sqrt(jnp.sum(kk * kk, -1, keepdims=True)), 1e-12)
    k = k * (1.0 + (a - 1.0) * lp['rwkv_k_a'])
    r_h, k_h, v_h, a_h = heads(r), heads(k), heads(v), heads(a)
    decay = jnp.exp(-jnp.exp(heads(w)))

    out, s_fin = wkv_scan(r_h, k_h, v_h, decay, kk, a_h, wkv0.astype(f32))
    mu = jnp.mean(out, -1, keepdims=True)
    var = jnp.mean(jnp.square(out - mu), -1, keepdims=True)
    on = ((out - mu) * lax.rsqrt(var + GN_EPS)).reshape(B, T, RWKV_WIDTH) * lp['rwkv_gn_w'] + lp['rwkv_gn_b']
    bonus = jnp.sum(r_h * k_h * lp['rwkv_r_k'], -1, keepdims=True) * v_h
    o = (on + bonus.reshape(B, T, RWKV_WIDTH)) * g
    return o.astype(zr.dtype), s_fin, zr[:, -1]


def nsa_project(zn, pos, lp):
    B, T, _ = zn.shape
    q = zn[..., :NSA_WIDTH].reshape(B, T, NSA_HEADS, HEAD_DIM)
    kv = zn[..., NSA_WIDTH:NSA_WIDTH + 6 * KV_WIDTH].reshape(B, T, N_NSA_BRANCH, 2, NSA_KV, HEAD_DIM)
    gates = jax.nn.sigmoid(zn[..., NSA_WIDTH + 6 * KV_WIDTH:].reshape(B, T, NSA_HEADS, N_NSA_BRANCH))
    qn = rmsnorm(q, lp['q_norm'])
    qr = rope(qn, pos)

    def prep(kvb, gk):
        return jnp.stack([rope(rmsnorm(kvb[:, :, 0], gk), pos), kvb[:, :, 1]], axis=2)

    return qn, qr, kv[:, :, 0], prep(kv[:, :, 1], lp['k_norm'][1]), prep(kv[:, :, 2], lp['k_norm'][2]), gates


SEL_COLS = 64
SEL_KT = 256
CHUNK_W = CMP_STRIDE * HEAD_DIM
NEG_BIG = -1e30


def _dot_nt(a, b):
    return lax.dot_general(a, b, (((1,), (1,)), ((), ())), preferred_element_type=jnp.float32)


def _split_bf16(x):
    top = lax.bitcast_convert_type(lax.bitcast_convert_type(x, jnp.uint32) & jnp.uint32(0xFFFF0000), jnp.float32)
    return top.astype(jnp.bfloat16), (x - top).astype(jnp.bfloat16)


def _dot3(a, b_hi, b_lo):
    a_hi, a_lo = _split_bf16(a)
    f32 = jnp.float32
    return (jnp.dot(a_hi, b_hi, preferred_element_type=f32) + jnp.dot(a_lo, b_hi, preferred_element_type=f32)
            + jnp.dot(a_hi, b_lo, preferred_element_type=f32))


def _compress_kernel(c_ref, cn_ref, w1h_ref, w1l_ref, pe_ref, w2h_ref, w2l_ref, kn_ref, kc_ref, vc_ref):
    for e in range(2):
        w1h, w1l = w1h_ref[e], w1l_ref[e]
        hid = (_dot3(c_ref[0, e, 0], w1h[:CHUNK_W], w1l[:CHUNK_W])
               + _dot3(cn_ref[0, e, 0], w1h[CHUNK_W:], w1l[CHUNK_W:]))
        pe_bias = _dot3(pe_ref[e], w1h, w1l)[0:1]
        hid = jax.nn.gelu(hid + pe_bias)
        out = _dot3(hid, w2h_ref[e], w2l_ref[e])
        if e == 0:
            out = out * lax.rsqrt(jnp.mean(out * out, -1, keepdims=True) + RMS_EPS) * kn_ref[...]
            kc_ref[0, 0] = out
        else:
            vc_ref[0, 0] = out


def compress_prompt(kv_cmp, lp):
    B, S = kv_cmp.shape[:2]
    nch = S // CMP_STRIDE
    c = kv_cmp.transpose(0, 2, 3, 1, 4).reshape(B, 2, NSA_KV, nch, CHUNK_W)
    cn = jnp.concatenate([c[:, :, :, 1:], jnp.zeros_like(c[:, :, :, :1])], axis=3)
    pe = jnp.broadcast_to(lp['cmp_pe'].reshape(2, 1, CMP_BLOCK * HEAD_DIM), (2, SUBLANES, CMP_BLOCK * HEAD_DIM))
    c_spec = pl.BlockSpec((1, 2, 1, nch, CHUNK_W), lambda b, k: (b, 0, k, 0, 0))
    full = lambda shape: pl.BlockSpec(shape, lambda b, k: (0,) * len(shape))
    o_spec = pl.BlockSpec((1, 1, nch, HEAD_DIM), lambda b, k: (b, k, 0, 0))
    w1_spec = full((2, CMP_BLOCK * HEAD_DIM, CMP_HIDDEN))
    w2_spec = full((2, CMP_HIDDEN, HEAD_DIM))
    return pl.pallas_call(
        _compress_kernel,
        grid=(B, NSA_KV),
        in_specs=[c_spec, c_spec, w1_spec, w1_spec, full((2, SUBLANES, CMP_BLOCK * HEAD_DIM)),
                  w2_spec, w2_spec, full((1, HEAD_DIM))],
        out_specs=[o_spec, o_spec],
        out_shape=[jax.ShapeDtypeStruct((B, NSA_KV, nch, HEAD_DIM), jnp.float32)] * 2,
        compiler_params=pltpu.CompilerParams(dimension_semantics=("arbitrary", "arbitrary")),
        name="nsa_compress",
    )(c, cn, *_split_bf16(lp['cmp_w1']), pe, *_split_bf16(lp['cmp_w2']), lp['k_norm'][0:1])


def _flash_update(q, kt, vt, valid, m, l, acc, scale):
    s = jnp.where(valid, _dot_nt(q, kt) * scale, NEG_BIG)
    m_new = jnp.maximum(m, jnp.max(s, -1, keepdims=True))
    p = jnp.where(valid, jnp.exp(s - m_new), 0.0)
    alpha = jnp.exp(m - m_new)
    l = alpha * l + jnp.sum(p, -1, keepdims=True)
    acc = alpha * acc + jnp.dot(p.astype(jnp.bfloat16), vt, preferred_element_type=jnp.float32)
    return m_new, l, acc


def _nsa_prompt_kernel(qn_ref, qr_ref, kc_ref, vc_ref, ks_ref, vs_ref, kw_ref, vw_ref, g_ref, o_ref):
    f32, bf16 = jnp.float32, jnp.bfloat16
    i = pl.program_id(2)
    t0 = i * Q_BLOCK
    scale = HEAD_DIM ** -0.5
    nch = kc_ref.shape[2]
    t_col = t0 + lax.broadcasted_iota(jnp.int32, (Q_BLOCK, 1), 0)

    kc = kc_ref[0, 0].astype(bf16)
    vc = vc_ref[0, 0].astype(bf16)
    n_row = lax.broadcasted_iota(jnp.int32, (1, nch), 1)
    c_valid = (n_row * CMP_STRIDE + (CMP_BLOCK - 1) <= t_col) & (n_row < nch - 1)
    p_sum = jnp.zeros((Q_BLOCK, nch), f32)
    o_cmp = []
    for g in range(NSA_GROUP):
        s = jnp.where(c_valid, _dot_nt(qn_ref[0, 0, g].astype(bf16), kc) * scale, -jnp.inf)
        m = jnp.max(s, -1, keepdims=True)
        m = jnp.where(m > -jnp.inf, m, 0.0)
        e = jnp.exp(s - m)
        p = e / jnp.maximum(jnp.sum(e, -1, keepdims=True), 1e-30)
        p_sum = p_sum + p
        o_cmp.append(jnp.dot(p.astype(bf16), vc, preferred_element_type=f32))

    n_col = lax.broadcasted_iota(jnp.int32, (nch, SEL_COLS), 0) * CMP_STRIDE
    j_blk = lax.broadcasted_iota(jnp.int32, (nch, SEL_COLS), 1) * SEL_BLOCK
    cover = ((n_col <= j_blk + (SEL_BLOCK - 1)) & (n_col + (CMP_BLOCK - 1) >= j_blk)).astype(bf16)
    p_hi = p_sum.astype(bf16)
    p_lo = (p_sum - p_hi.astype(f32)).astype(bf16)
    imp = jnp.dot(p_hi, cover, preferred_element_type=f32) + jnp.dot(p_lo, cover, preferred_element_type=f32)
    j_row = lax.broadcasted_iota(jnp.int32, (1, SEL_COLS), 1)
    cur = t_col // SEL_BLOCK
    forced = (j_row == cur) | (j_row == cur - 1) | (j_row == 0)
    imp = jnp.where(forced, jnp.inf, jnp.where(j_row > cur, -jnp.inf, imp))
    rank = jnp.zeros((Q_BLOCK, SEL_COLS), f32)
    for jp in range(SEL_COLS):
        col = imp[:, jp:jp + 1]
        rank = rank + jnp.where((col > imp) | ((col == imp) & (jp < j_row)), 1.0, 0.0)
    sel_mask = jnp.where(rank < N_SELECT, 1.0, 0.0).astype(bf16)

    q_rot = [qr_ref[0, 0, g].astype(bf16) for g in range(NSA_GROUP)]

    def init():
        return tuple((jnp.full((Q_BLOCK, 1), NEG_BIG, f32), jnp.zeros((Q_BLOCK, 1), f32),
                      jnp.zeros((Q_BLOCK, HEAD_DIM), f32)) for _ in range(NSA_GROUP))

    def sel_body(c, carry):
        k0 = pl.multiple_of(c * SEL_KT, SEL_KT)
        kt = ks_ref[0, 0, pl.ds(k0, SEL_KT), :].astype(bf16)
        vt = vs_ref[0, 0, pl.ds(k0, SEL_KT), :].astype(bf16)
        kpos = k0 + lax.broadcasted_iota(jnp.int32, (1, SEL_KT), 1)
        expand = (lax.broadcasted_iota(jnp.int32, (SEL_COLS, SEL_KT), 0)
                  == (k0 + lax.broadcasted_iota(jnp.int32, (SEL_COLS, SEL_KT), 1)) // SEL_BLOCK).astype(bf16)
        valid = (jnp.dot(sel_mask, expand, preferred_element_type=f32) > 0.5) & (kpos <= t_col)
        return tuple(_flash_update(q_rot[g], kt, vt, valid, *carry[g], scale) for g in range(NSA_GROUP))

    sel = lax.fori_loop(0, i // (SEL_KT // Q_BLOCK) + 1, sel_body, init())

    def win_body(c, carry):
        k0 = pl.multiple_of(c * SEL_KT, SEL_KT)
        kt = kw_ref[0, 0, pl.ds(k0, SEL_KT), :].astype(bf16)
        vt = vw_ref[0, 0, pl.ds(k0, SEL_KT), :].astype(bf16)
        dist = t_col - (k0 + lax.broadcasted_iota(jnp.int32, (1, SEL_KT), 1))
        valid = (dist >= 0) & (dist < WINDOW)
        return tuple(_flash_update(q_rot[g], kt, vt, valid, *carry[g], scale) for g in range(NSA_GROUP))

    last_tile = i // (SEL_KT // Q_BLOCK)
    win = lax.fori_loop(jnp.maximum(last_tile - WINDOW // SEL_KT, 0), last_tile + 1, win_body, init())

    for g in range(NSA_GROUP):
        gate = g_ref[0, 0, g]
        o_sel = sel[g][2] / jnp.maximum(sel[g][1], 1e-30)
        o_win = win[g][2] / jnp.maximum(win[g][1], 1e-30)
        o_ref[0, 0, g] = o_cmp[g] * gate[:, 0:1] + o_sel * gate[:, 1:2] + o_win * gate[:, 2:3]


def nsa_prompt(qn, qr, kv_cmp, kv_sel, kv_win, gates, lp):
    B, S = qn.shape[:2]
    assert S % SEL_KT == 0 and S <= SEL_COLS * SEL_BLOCK
    nch = S // CMP_STRIDE
    kc, vc = compress_prompt(kv_cmp, lp)
    heads = lambda t: t.reshape(B, S, NSA_KV, NSA_GROUP, t.shape[-1]).transpose(0, 2, 3, 1, 4)
    kv_major = lambda t: t.transpose(0, 2, 1, 3)
    q_spec = pl.BlockSpec((1, 1, NSA_GROUP, Q_BLOCK, HEAD_DIM), lambda b, k, i: (b, k, 0, i, 0))
    g_spec = pl.BlockSpec((1, 1, NSA_GROUP, Q_BLOCK, N_NSA_BRANCH), lambda b, k, i: (b, k, 0, i, 0))
    c_spec = pl.BlockSpec((1, 1, nch, HEAD_DIM), lambda b, k, i: (b, k, 0, 0))
    s_spec = pl.BlockSpec((1, 1, S, HEAD_DIM), lambda b, k, i: (b, k, 0, 0))
    o = pl.pallas_call(
        _nsa_prompt_kernel,
        grid=(B, NSA_KV, S // Q_BLOCK),
        in_specs=[q_spec, q_spec, c_spec, c_spec, s_spec, s_spec, s_spec, s_spec, g_spec],
        out_specs=q_spec,
        out_shape=jax.ShapeDtypeStruct((B, NSA_KV, NSA_GROUP, S, HEAD_DIM), jnp.float32),
        compiler_params=pltpu.CompilerParams(dimension_semantics=("arbitrary", "arbitrary", "arbitrary")),
        name="nsa_prompt",
    )(heads(qn), heads(qr), kc, vc, kv_major(kv_sel[:, :, 0]), kv_major(kv_sel[:, :, 1]),
      kv_major(kv_win[:, :, 0]), kv_major(kv_win[:, :, 1]), heads(gates))
    return o.transpose(0, 3, 1, 2, 4).reshape(B, S, NSA_WIDTH)


PAGE_GROUP = 16
PAGE_CHUNKS = PAGE_SIZE // CMP_STRIDE
KV_ROW = 2 * KV_WIDTH
SAMPLE_ROWS = NSA_HEADS * DEC_SEQ
SEL_COLS_S = 256


def _page_specs(block, n_pages):
    zeros = (0,) * (len(block) - 1)
    return [pl.BlockSpec(block, (lambda b, c, pt, r=r: (pt[b * n_pages + c * PAGE_GROUP + r],) + zeros))
            for r in range(PAGE_GROUP)]


def _compress_sample_kernel(pt_ref, *refs):
    f32, bf16 = jnp.float32, jnp.bfloat16
    pages = refs[:PAGE_GROUP]
    w1h_ref, w1l_ref, pe_ref, w2h_ref, w2l_ref, kn_ref, kc_ref, vc_ref, p0_sc, p1_sc = refs[PAGE_GROUP:]
    c = pl.program_id(1)
    rows = PAGE_GROUP * PAGE_CHUNKS
    r0 = pl.multiple_of(c * rows, rows)
    for e in range(2):
        cm = jnp.concatenate([pg[0, e, k] for k in range(NSA_KV) for pg in pages], axis=0)
        c_hi, c_lo = _split_bf16(cm)
        for part, dst in enumerate((p0_sc, p1_sc)):
            w_hi = w1h_ref[e, part * CHUNK_W:(part + 1) * CHUNK_W, :]
            w_lo = w1l_ref[e, part * CHUNK_W:(part + 1) * CHUNK_W, :]
            y = (jnp.dot(c_hi, w_hi, preferred_element_type=f32) + jnp.dot(c_lo, w_hi, preferred_element_type=f32)
                 + jnp.dot(c_hi, w_lo, preferred_element_type=f32))
            for k in range(NSA_KV):
                dst[e, k, pl.ds(r0, rows), :] = y[k * rows:(k + 1) * rows]

    @pl.when(c == pl.num_programs(1) - 1)
    def _():
        nch = p0_sc.shape[2]
        for e in range(2):
            pe_bias = _dot3(pe_ref[e], w1h_ref[e], w1l_ref[e])[0:1]
            out = jnp.zeros((nch, LANES), f32)
            for k in range(NSA_KV):
                nxt = pltpu.roll(p1_sc[e, k], nch - 1, 0)
                hid = jax.nn.gelu(p0_sc[e, k] + nxt + pe_bias)
                out = out + _dot3(hid, w2h_ref[e, k], w2l_ref[e, k])
            if e == 0:
                lane = lax.broadcasted_iota(jnp.int32, (1, LANES), 1)
                sq = out * out
                ms0 = jnp.sum(jnp.where(lane < HEAD_DIM, sq, 0.0), -1, keepdims=True) / HEAD_DIM
                ms1 = jnp.sum(jnp.where(lane >= HEAD_DIM, sq, 0.0), -1, keepdims=True) / HEAD_DIM
                kc_ref[0] = out * jnp.where(lane < HEAD_DIM, lax.rsqrt(ms0 + RMS_EPS), lax.rsqrt(ms1 + RMS_EPS)) * kn_ref[...]
            else:
                vc_ref[0] = out


def _nsa_sample_kernel(pt_ref, *refs):
    f32, bf16 = jnp.float32, jnp.bfloat16
    pages = refs[:PAGE_GROUP]
    (qn_ref, qr_ref, kc_ref, vc_ref, knew_ref, win_ref, g_ref, o_ref,
     m_sc, l_sc, acc_sc, selm_sc, ocmp_sc) = refs[PAGE_GROUP:]
    c = pl.program_id(1)
    scale = HEAD_DIM ** -0.5
    T = DEC_SEQ
    nch = kc_ref.shape[1]
    keys = PAGE_GROUP * PAGE_SIZE
    t_row = lax.broadcasted_iota(jnp.int32, (SAMPLE_ROWS, 1), 0) % T
    q_rot = qr_ref[0].astype(bf16)

    @pl.when(c == 0)
    def _():
        n_row = lax.broadcasted_iota(jnp.int32, (1, nch), 1)
        s = jnp.where(n_row < nch - 1, _dot_nt(qn_ref[0].astype(bf16), kc_ref[0].astype(bf16)) * scale, -jnp.inf)
        e = jnp.exp(s - jnp.max(s, -1, keepdims=True))
        p = e / jnp.maximum(jnp.sum(e, -1, keepdims=True), 1e-30)
        ocmp_sc[...] = jnp.dot(p.astype(bf16), vc_ref[0].astype(bf16), preferred_element_type=f32)
        grp = NSA_GROUP * T
        p_sum = jnp.concatenate(
            [sum(p[k * grp + g * T:k * grp + (g + 1) * T] for g in range(NSA_GROUP)) for k in range(NSA_KV)], axis=0)
        n_col = lax.broadcasted_iota(jnp.int32, (nch, SEL_COLS_S), 0) * CMP_STRIDE
        j_blk = lax.broadcasted_iota(jnp.int32, (nch, SEL_COLS_S), 1) * SEL_BLOCK
        cover = ((n_col <= j_blk + (SEL_BLOCK - 1)) & (n_col + (CMP_BLOCK - 1) >= j_blk)).astype(bf16)
        p_hi = p_sum.astype(bf16)
        p_lo = (p_sum - p_hi.astype(f32)).astype(bf16)
        imp = jnp.dot(p_hi, cover, preferred_element_type=f32) + jnp.dot(p_lo, cover, preferred_element_type=f32)
        j_row = lax.broadcasted_iota(jnp.int32, (1, SEL_COLS_S), 1)
        cur = (PAST_LEN + lax.broadcasted_iota(jnp.int32, (NSA_KV * T, 1), 0) % T) // SEL_BLOCK
        forced = (j_row == cur) | (j_row == cur - 1) | (j_row == 0)
        imp = jnp.where(forced, jnp.inf, jnp.where(j_row > cur, -jnp.inf, imp))
        rank = jnp.zeros((NSA_KV * T, SEL_COLS_S), f32)
        for jp in range(-(-(PAST_LEN + T) // SEL_BLOCK)):
            col = imp[:, jp:jp + 1]
            rank = rank + jnp.where((col > imp) | ((col == imp) & (jp < j_row)), 1.0, 0.0)
        sel16 = jnp.where(rank < N_SELECT, 1.0, 0.0)
        selm_sc[...] = jnp.concatenate([sel16[k * T:(k + 1) * T] for k in range(NSA_KV) for _ in range(NSA_GROUP)], axis=0)
        m_sc[...] = jnp.full(m_sc.shape, NEG_BIG, f32)
        l_sc[...] = jnp.zeros(l_sc.shape, f32)
        acc_sc[...] = jnp.zeros(acc_sc.shape, f32)

    def update(kt, vt, valid):
        m, l, acc = _flash_update(q_rot, kt, vt, valid, m_sc[:, 0:1], l_sc[:, 0:1], acc_sc[...], scale)
        m_sc[...] = jnp.broadcast_to(m, m_sc.shape)
        l_sc[...] = jnp.broadcast_to(l, l_sc.shape)
        acc_sc[...] = acc

    kt = jnp.concatenate([pg[0, :, 0:KV_WIDTH] for pg in pages], axis=0).astype(bf16)
    vt = jnp.concatenate([pg[0, :, KV_WIDTH:KV_ROW] for pg in pages], axis=0).astype(bf16)
    past_cols = PAST_LEN // SEL_BLOCK
    expand = (lax.broadcasted_iota(jnp.int32, (past_cols, keys), 0)
              == c * (keys // SEL_BLOCK) + lax.broadcasted_iota(jnp.int32, (past_cols, keys), 1) // SEL_BLOCK).astype(bf16)
    update(kt, vt, jnp.dot(selm_sc[:, 0:past_cols].astype(bf16), expand, preferred_element_type=f32) > 0.5)

    @pl.when(c == pl.num_programs(1) - 1)
    def _():
        s_idx = lax.broadcasted_iota(jnp.int32, (1, LANES), 1)
        member = selm_sc[:, past_cols:past_cols + 1] > 0.5
        update(knew_ref[0, :, 0:KV_WIDTH].astype(bf16), knew_ref[0, :, KV_WIDTH:KV_ROW].astype(bf16),
               member & (s_idx <= t_row) & (s_idx < T))
        o_sel = acc_sc[...] / jnp.maximum(l_sc[:, 0:1], 1e-30)
        n_win = win_ref.shape[1]
        win_cache = min(WINDOW, PAST_LEN)
        w_idx = lax.broadcasted_iota(jnp.int32, (1, n_win), 1)
        dist = (PAST_LEN + t_row) - (PAST_LEN - win_cache + w_idx)
        valid = (dist >= 0) & (dist < WINDOW) & (w_idx < win_cache + T)
        m, l, acc = _flash_update(q_rot, win_ref[0, :, 0:KV_WIDTH].astype(bf16), win_ref[0, :, KV_WIDTH:KV_ROW].astype(bf16),
                                  valid, jnp.full((SAMPLE_ROWS, 1), NEG_BIG, f32), jnp.zeros((SAMPLE_ROWS, 1), f32),
                                  jnp.zeros((SAMPLE_ROWS, LANES), f32), scale)
        o_win = acc / jnp.maximum(l, 1e-30)
        o = (ocmp_sc[...] * g_ref[0, :, 0:LANES] + o_sel * g_ref[0, :, LANES:2 * LANES]
             + o_win * g_ref[0, :, 2 * LANES:3 * LANES])
        first = lax.broadcasted_iota(jnp.int32, (SAMPLE_ROWS, 1), 0) < SAMPLE_ROWS // NSA_KV
        o_ref[0] = jnp.where(first, o, pltpu.roll(o, HEAD_DIM, 1))


def nsa_sample(qn, qr, kv_cmp, kv_sel, kv_win, gates, cache_cmp, cache_sel, page_table, state_win, lp):
    B, T = qn.shape[:2]
    n_pages = PAST_LEN // PAGE_SIZE
    n_phys = cache_cmp.shape[0]
    nch = PAST_LEN // CMP_STRIDE
    assert T == DEC_SEQ == SUBLANES and (PAST_LEN + T) // CMP_STRIDE == nch and n_pages % PAGE_GROUP == 0
    assert -(-(PAST_LEN + T) // SEL_BLOCK) <= SEL_COLS_S
    pt = page_table.reshape(B * n_pages)
    n_steps = n_pages // PAGE_GROUP
    const = lambda shape: pl.BlockSpec(shape, lambda b, c, pt: (0,) * len(shape))
    per_seq = lambda shape: pl.BlockSpec((1,) + shape, lambda b, c, pt: (b,) + (0,) * len(shape))
    params = pltpu.CompilerParams(dimension_semantics=("arbitrary", "arbitrary"))

    cmp_pages = cache_cmp.reshape(n_phys, PAGE_CHUNKS, CMP_STRIDE, 2, NSA_KV, HEAD_DIM).transpose(0, 3, 4, 1, 2, 5)
    cmp_pages = cmp_pages.reshape(n_phys, 2, NSA_KV, PAGE_CHUNKS, CHUNK_W)
    pe = jnp.broadcast_to(lp['cmp_pe'].reshape(2, 1, CMP_BLOCK * HEAD_DIM), (2, SUBLANES, CMP_BLOCK * HEAD_DIM))
    w2 = lp['cmp_w2']
    zero = jnp.zeros_like(w2)
    w2_half = jnp.stack([jnp.concatenate([w2, zero], -1), jnp.concatenate([zero, w2], -1)], axis=1)
    kn = jnp.tile(lp['k_norm'][0:1], (1, NSA_KV))
    kc, vc = pl.pallas_call(
        _compress_sample_kernel,
        grid_spec=pltpu.PrefetchScalarGridSpec(
            num_scalar_prefetch=1, grid=(B, n_steps),
            in_specs=_page_specs((1, 2, NSA_KV, PAGE_CHUNKS, CHUNK_W), n_pages)
            + [const((2, CMP_BLOCK * HEAD_DIM, CMP_HIDDEN))] * 2 + [const((2, SUBLANES, CMP_BLOCK * HEAD_DIM))]
            + [const((2, NSA_KV, CMP_HIDDEN, LANES))] * 2 + [const((1, LANES))],
            out_specs=[per_seq((nch, LANES)), per_seq((nch, LANES))],
            scratch_shapes=[pltpu.VMEM((2, NSA_KV, nch, CMP_HIDDEN), jnp.float32)] * 2),
        out_shape=[jax.ShapeDtypeStruct((B, nch, LANES), jnp.float32)] * 2,
        compiler_params=params,
        name="nsa_compress_sample",
    )(pt, *([cmp_pages] * PAGE_GROUP), *_split_bf16(lp['cmp_w1']), pe, *_split_bf16(w2_half), kn)

    def rows(t):
        return t.reshape(B, T, NSA_KV, NSA_GROUP, t.shape[-1]).transpose(0, 2, 3, 1, 4).reshape(B, SAMPLE_ROWS, t.shape[-1])

    def lane_halves(q):
        first = (jnp.arange(SAMPLE_ROWS) < SAMPLE_ROWS // NSA_KV)[None, :, None]
        return jnp.concatenate([jnp.where(first, q, 0.0), jnp.where(first, 0.0, q)], -1)

    flat = lambda t: t.reshape(t.shape[0], t.shape[1], KV_ROW)
    pad_rows = lambda t, n: jnp.pad(t, ((0, 0), (0, n - t.shape[1]), (0, 0)))
    k_new = pad_rows(flat(kv_sel), LANES)
    win_cache = state_win.shape[1]
    n_win = -(-(win_cache + T) // LANES) * LANES
    win_all = pad_rows(jnp.concatenate([flat(state_win), flat(kv_win)], 1), n_win)
    sel_pages = cache_sel.reshape(n_phys, PAGE_SIZE, KV_ROW)
    o = pl.pallas_call(
        _nsa_sample_kernel,
        grid_spec=pltpu.PrefetchScalarGridSpec(
            num_scalar_prefetch=1, grid=(B, n_steps),
            in_specs=_page_specs((1, PAGE_SIZE, KV_ROW), n_pages)
            + [per_seq((SAMPLE_ROWS, LANES)), per_seq((SAMPLE_ROWS, LANES)), per_seq((nch, LANES)), per_seq((nch, LANES)),
               per_seq((LANES, KV_ROW)), per_seq((n_win, KV_ROW)), per_seq((SAMPLE_ROWS, N_NSA_BRANCH * LANES))],
            out_specs=per_seq((SAMPLE_ROWS, LANES)),
            scratch_shapes=[pltpu.VMEM((SAMPLE_ROWS, LANES), jnp.float32)] * 3
            + [pltpu.VMEM((SAMPLE_ROWS, SEL_COLS_S), jnp.float32), pltpu.VMEM((SAMPLE_ROWS, LANES), jnp.float32)]),
        out_shape=jax.ShapeDtypeStruct((B, SAMPLE_ROWS, LANES), jnp.float32),
        compiler_params=params,
        name="nsa_sample",
    )(pt, *([sel_pages] * PAGE_GROUP), lane_halves(rows(qn)), lane_halves(rows(qr)), kc, vc, k_new, win_all,
      jnp.repeat(rows(gates), LANES, axis=-1))
    o = o[:, :, :HEAD_DIM]
    return o.reshape(B, NSA_KV, NSA_GROUP, T, HEAD_DIM).transpose(0, 3, 1, 2, 4).reshape(B, T, NSA_WIDTH)


PEER_SLOTS = PEER_HEADS * PEER_TOPK
PEER_TOK = 128
HALF_D = D_MODEL // 2
ROW_SUB = HALF_D // LANES


def pack_expert_table(tab):
    bits = lax.bitcast_convert_type(tab.astype(jnp.bfloat16), jnp.uint16).astype(jnp.uint32)
    return ((bits[:, HALF_D:] << 16) | bits[:, :HALF_D]).reshape(tab.shape[0], ROW_SUB, LANES)


def _unpack_row(row):
    lo = lax.bitcast_convert_type(row << 16, jnp.float32)
    hi = lax.bitcast_convert_type(row & jnp.uint32(0xFFFF0000), jnp.float32)
    return lo, hi


def _load_table_and_indices(idx_hbm, tab_hbm, tab_vmem, idx_smem, sem):
    i = pl.program_id(0)

    @pl.when(i == 0)
    def _():
        cp = pltpu.make_async_copy(tab_hbm, tab_vmem, sem.at[0])
        cp.start()
        cp.wait()

    n = idx_smem.shape[0]
    cp = pltpu.make_async_copy(idx_hbm.at[pl.ds(i * n, n)], idx_smem, sem.at[1])
    cp.start()
    cp.wait()


def _eye():
    return (lax.broadcasted_iota(jnp.int32, (PEER_SLOTS, LANES), 0)
            == lax.broadcasted_iota(jnp.int32, (PEER_SLOTS, LANES), 1))


def _peer_act_kernel(idx_hbm, h_ref, gate_ref, tab_hbm, w_ref, tab_vmem, idx_smem, ps_ref, a_ref, sem):
    _load_table_and_indices(idx_hbm, tab_hbm, tab_vmem, idx_smem, sem)
    eye = _eye()

    def tok(t, carry):
        h_lo = h_ref[t, 0:ROW_SUB, :]
        h_hi = h_ref[t, ROW_SUB:2 * ROW_SUB, :]
        for j in range(PEER_SLOTS):
            lo, hi = _unpack_row(tab_vmem[idx_smem[t * PEER_SLOTS + j]])
            ps_ref[j * ROW_SUB:(j + 1) * ROW_SUB, :] = lo * h_lo + hi * h_hi
        q = ps_ref[pl.ds(0, PEER_SLOTS, stride=ROW_SUB), :]
        for s in range(1, ROW_SUB):
            q = q + ps_ref[pl.ds(s, PEER_SLOTS, stride=ROW_SUB), :]
        a_col = jnp.sum(q, axis=-1, keepdims=True)
        a_ref[pl.ds(t, 1), :] = jnp.sum(jnp.where(eye, a_col, 0.0), axis=0, keepdims=True)
        return carry

    lax.fori_loop(0, h_ref.shape[0], tok, 0)
    w_ref[...] = gate_ref[...] * jax.nn.gelu(a_ref[...])


def _peer_out_kernel(idx_hbm, w_ref, tab_hbm, o_ref, tab_vmem, idx_smem, wb_ref, sem):
    _load_table_and_indices(idx_hbm, tab_hbm, tab_vmem, idx_smem, sem)
    eye = _eye()
    n_acc = 4

    def tok(t, carry):
        w_row = w_ref[pl.ds(t, 1), :]
        w_col = jnp.sum(jnp.where(eye, w_row, 0.0), axis=-1, keepdims=True)
        wb_ref[...] = jnp.broadcast_to(w_col, (PEER_SLOTS, LANES))
        acc_lo = [jnp.zeros((ROW_SUB, LANES), jnp.float32) for _ in range(n_acc)]
        acc_hi = [jnp.zeros((ROW_SUB, LANES), jnp.float32) for _ in range(n_acc)]
        for j in range(PEER_SLOTS):
            lo, hi = _unpack_row(tab_vmem[idx_smem[t * PEER_SLOTS + j]])
            wj = wb_ref[j:j + 1, :]
            acc_lo[j % n_acc] = acc_lo[j % n_acc] + wj * lo
            acc_hi[j % n_acc] = acc_hi[j % n_acc] + wj * hi
        o_ref[t, 0:ROW_SUB, :] = (acc_lo[0] + acc_lo[1]) + (acc_lo[2] + acc_lo[3])
        o_ref[t, ROW_SUB:2 * ROW_SUB, :] = (acc_hi[0] + acc_hi[1]) + (acc_hi[2] + acc_hi[3])
        return carry

    lax.fori_loop(0, w_ref.shape[0], tok, 0)


def peer_experts(ht, e, gate, u_packed, v_packed):
    n = ht.shape[0]
    n_exp = u_packed.shape[0]
    tb = min(PEER_TOK, n)
    assert n % tb == 0 and tb % SUBLANES == 0
    idx =e.reshape(n * PEER_SLOTS)
    h3 = ht.reshape(n, 2 * ROW_SUB, LANES)
    any_spec = pl.BlockSpec(memory_space=pl.ANY)
    tok_spec = pl.BlockSpec((tb, PEER_SLOTS), lambda i: (i, 0))
    row_spec = pl.BlockSpec((tb, 2 * ROW_SUB, LANES), lambda i: (i, 0, 0))
    params = pltpu.CompilerParams(dimension_semantics=("arbitrary",), vmem_limit_bytes=VMEM_TABLE_LIMIT)
    table = pltpu.VMEM((n_exp, ROW_SUB, LANES), jnp.uint32)
    idx_smem = pltpu.SMEM((tb * PEER_SLOTS,), jnp.int32)
    w = pl.pallas_call(
        _peer_act_kernel,
        grid=(n // tb,),
        in_specs=[any_spec, row_spec, tok_spec, any_spec],
        out_specs=tok_spec,
        out_shape=jax.ShapeDtypeStruct((n, PEER_SLOTS), jnp.float32),
        scratch_shapes=[table, idx_smem, pltpu.VMEM((PEER_SLOTS * ROW_SUB, LANES), jnp.float32),
                        pltpu.VMEM((tb, PEER_SLOTS), jnp.float32), pltpu.SemaphoreType.DMA((2,))],
        compiler_params=params,
        name="peer_act",
    )(idx, h3, gate, u_packed)
    out = pl.pallas_call(
        _peer_out_kernel,
        grid=(n // tb,),
        in_specs=[any_spec, tok_spec, any_spec],
        out_specs=row_spec,
        out_shape=jax.ShapeDtypeStruct((n, 2 * ROW_SUB, LANES), jnp.float32),
        scratch_shapes=[table, idx_smem, pltpu.VMEM((PEER_SLOTS, LANES), jnp.float32),
                        pltpu.SemaphoreType.DMA((2,))],
        compiler_params=params,
        name="peer_out",
    )(idx, w, v_packed)
    return out.reshape(n, D_MODEL)


def _top_rows(s, row, n_top, val_sc, idx_sc, base):
    for r in range(n_top):
        m = jnp.max(s, axis=0, keepdims=True)
        idx = jnp.min(jnp.where(s == m, row, float(1 << 20)), axis=0, keepdims=True)
        s = jnp.where(row == idx, -jnp.inf, s)
        val_sc[pl.ds(base + r, 1), :] = m
        idx_sc[pl.ds(base + r, 1), :] = idx


def _peer_route_kernel(h_ref, wq_ref, khi_ref, klo_ref, e_ref, g_ref, sv_sc, si_sc, cv_sc, ci_sc, eo_sc, go_sc):
    f32, bf16 = jnp.float32, jnp.bfloat16
    tb = h_ref.shape[0]
    k = PEER_TOPK
    h_bf = h_ref[...].astype(bf16)
    key_row = lax.broadcasted_iota(jnp.int32, (N_KEYS, tb), 0).astype(f32)
    n_pair = k + (k - 1) * (k // 2)
    rr = lax.broadcasted_iota(jnp.int32, (n_pair, tb), 0)
    pair_a = jnp.where(rr < k, 0, 1 + (rr - k) // (k // 2))
    pair_b = jnp.where(rr < k, rr, (rr - k) % (k // 2))
    pair_ok = (pair_a + 1) * (pair_b + 1) <= k
    pair_row = (pair_a * k + pair_b).astype(f32)

    def head(hh, carry):
        col = pl.multiple_of(hh * PEER_DK, PEER_DK)
        q = jnp.dot(h_bf, wq_ref[:, pl.ds(col, PEER_DK)], preferred_element_type=f32)
        q_hi, q_lo = _split_bf16(q)
        for c in range(2):
            k_hi, k_lo = khi_ref[hh, c], klo_ref[hh, c]
            s = _dot_nt(k_hi, q_hi) + _dot_nt(k_lo, q_hi) + _dot_nt(k_hi, q_lo)
            _top_rows(s, key_row, k, sv_sc, si_sc, c * k)
        sv0, sv1 = sv_sc[0:k, :], sv_sc[k:2 * k, :]
        si0, si1 = si_sc[0:k, :], si_sc[k:2 * k, :]
        cand = jnp.concatenate([sv0[0:1, :] + sv1] + [sv0[a:a + 1, :] + sv1[0:k // 2, :] for a in range(1, k)], axis=0)
        cand = jnp.where(pair_ok, cand, -jnp.inf)
        ids = jnp.concatenate([si0[0:1, :] * float(N_KEYS) + si1]
                              + [si0[a:a + 1, :] * float(N_KEYS) + si1[0:k // 2, :] for a in range(1, k)], axis=0)
        _top_rows(cand, pair_row, k, cv_sc, ci_sc, 0)
        ci = ci_sc[...]
        base = pl.multiple_of(hh * k, k)
        for r in range(k):
            eo_sc[pl.ds(base + r, 1), :] = jnp.sum(jnp.where(pair_row == ci[r:r + 1, :], ids, 0.0), axis=0, keepdims=True)
        ex = jnp.exp(cv_sc[...] - cv_sc[0:1, :])
        go_sc[pl.ds(base, k), :] = ex / jnp.sum(ex, axis=0, keepdims=True)
        return carry

    lax.fori_loop(0, PEER_HEADS, head, 0)
    e_ref[...] = eo_sc[...].T.astype(jnp.int32)
    g_ref[...] = go_sc[...].T


def peer_route(ht, lp):
    n = ht.shape[0]
    tb = min(PEER_TOK, n)
    assert n % tb == 0 and tb == PEER_SLOTS
    half = PEER_DK // 2
    keys = lp['peer_keys']
    zero = jnp.zeros_like(keys[:, 0])
    keys = jnp.stack([jnp.concatenate([keys[:, 0], zero], -1), jnp.concatenate([zero, keys[:, 1]], -1)], axis=1)
    k_hi, k_lo = _split_bf16(keys)
    const = lambda shape: pl.BlockSpec(shape, lambda i: (0,) * len(shape))
    tok_spec = pl.BlockSpec((tb, PEER_SLOTS), lambda i: (i, 0))
    return pl.pallas_call(
        _peer_route_kernel,
        grid=(n // tb,),
        in_specs=[pl.BlockSpec((tb, D_MODEL), lambda i: (i, 0)), const((D_MODEL, PEER_HEADS * PEER_DK)),
                  const(keys.shape), const(keys.shape)],
        out_specs=[tok_spec, tok_spec],
        out_shape=[jax.ShapeDtypeStruct((n, PEER_SLOTS), jnp.int32), jax.ShapeDtypeStruct((n, PEER_SLOTS), jnp.float32)],
        scratch_shapes=[pltpu.VMEM((2 * PEER_TOPK, tb), jnp.float32)] * 2 + [pltpu.VMEM((PEER_TOPK, tb), jnp.float32)] * 2
        + [pltpu.VMEM((PEER_SLOTS, tb), jnp.float32)] * 2,
        compiler_params=pltpu.CompilerParams(dimension_semantics=("arbitrary",)),
        name="peer_route",
    )(ht, lp['peer_wq'].astype(jnp.bfloat16), k_hi, k_lo)


def peer_ffn(h, lp):
    lead = h.shape[:-1]
    ht = h.reshape(-1, D_MODEL)
    e, gate = peer_route(ht, lp)
    out = peer_experts(ht, e, gate, lp['peer_u_packed'], lp['peer_v_packed'])
    return out.reshape(*lead, D_MODEL)


MATMUL_OUT_BYTES = 6 * 1024 * 1024
MATMUL_VMEM_LIMIT = 48 * 1024 * 1024


def _matmul_kernel(x_ref, w_ref, o_ref):
    o_ref[...] = jnp.dot(x_ref[...].astype(jnp.bfloat16), w_ref[...], preferred_element_type=jnp.float32)


def _norm_matmul_kernel(x_ref, g_ref, w_ref, o_ref):
    x = x_ref[...]
    h = x * lax.rsqrt(jnp.mean(x * x, -1, keepdims=True) + RMS_EPS) * g_ref[...]
    o_ref[...] = jnp.dot(h.astype(jnp.bfloat16), w_ref[...], preferred_element_type=jnp.float32)


def dense(x, w, gain=None):
    lead, k = x.shape[:-1], x.shape[-1]
    n = w.shape[1]
    n_pad = -(-n // LANES) * LANES
    x2 = x.reshape(-1, k)
    m = x2.shape[0]
    tm = m
    while tm * n_pad * 4 > MATMUL_OUT_BYTES and tm % (2 * SUBLANES) == 0:
        tm //= 2
    w_bf = jnp.pad(w, ((0, 0), (0, n_pad - n))).astype(jnp.bfloat16)
    x_spec = pl.BlockSpec((tm, k), lambda i: (i, 0))
    w_spec = pl.BlockSpec((k, n_pad), lambda i: (0, 0))
    o_spec = pl.BlockSpec((tm, n_pad), lambda i: (i, 0))
    params = pltpu.CompilerParams(dimension_semantics=("arbitrary",), vmem_limit_bytes=MATMUL_VMEM_LIMIT)
    out_shape = jax.ShapeDtypeStruct((m, n_pad), jnp.float32)
    if gain is None:
        out = pl.pallas_call(_matmul_kernel, grid=(m // tm,), in_specs=[x_spec, w_spec], out_specs=o_spec,
                             out_shape=out_shape, compiler_params=params, name="dense")(x2, w_bf)
    else:
        g_spec = pl.BlockSpec((1, k), lambda i: (0, 0))
        out = pl.pallas_call(_norm_matmul_kernel, grid=(m // tm,), in_specs=[x_spec, g_spec, w_spec], out_specs=o_spec,
                             out_shape=out_shape, compiler_params=params, name="norm_dense")(x2, gain.reshape(1, k), w_bf)
    return out[:, :n].reshape(*lead, n)


def layer_forward(x, p_l, pos, z_prev0, wkv0, attend, lp):
    z = dense(x, lp['w_in'], gain=lp['attn_norm'])
    o_r, wkv, shift = rwkv_mix(z[..., :RWKV_COLS], z_prev0, wkv0, lp)
    qn, qr, kvc, kvs, kvw, gates = nsa_project(z[..., RWKV_COLS:RWKV_COLS + NSA_COLS], pos, lp)
    o_n = attend(qn, qr, kvc, kvs, kvw, gates)
    zg = z[..., RWKV_COLS + NSA_COLS:]
    merged = (jax.nn.sigmoid(zg[..., :D_MODEL]) * dense(o_r, lp['w_rwkv_out'])
              + jax.nn.sigmoid(zg[..., D_MODEL:]) * dense(o_n, lp['w_nsa_out']))
    x = x + dense(merged, lp['w_out'])
    x = x + peer_ffn(rmsnorm(x, lp['ffn_norm']), lp)
    x = x + jax.nn.sigmoid(dense(x, lp['ple_w_gate'], gain=lp['ple_norm'])) * dense(p_l, lp['ple_w_in'])
    return x, kvc, kvs, kvw, wkv, shift


def kernel(x_prompt, x_sample, cache_cmp_kv, cache_sel_kv, state_win_kv, state_wkv, state_shift, page_table,
           p_prompt, p_sample, attn_norm, w_in, rwkv_mu, rwkv_w0, rwkv_w_up, rwkv_a0, rwkv_a_up, rwkv_g_up,
           rwkv_k_k, rwkv_k_a, rwkv_r_k, rwkv_gn_w, rwkv_gn_b, w_rwkv_out, q_norm, k_norm, cmp_pe, cmp_w1,
           cmp_w2, w_nsa_out, w_out, ffn_norm, peer_wq, peer_keys, peer_u, peer_v, ple_norm, ple_w_gate,
           ple_w_in):
    b_p, s_p = x_prompt.shape[:2]
    pos_p = jnp.arange(s_p)
    pos_s = PAST_LEN + jnp.arange(x_sample.shape[1])
    xp, xs = x_prompt, x_sample
    l = 0
    lp = dict(attn_norm=attn_norm[l], w_in=w_in[l], rwkv_mu=rwkv_mu[l], rwkv_w0=rwkv_w0[l],
              rwkv_w_up=rwkv_w_up[l], rwkv_a0=rwkv_a0[l], rwkv_a_up=rwkv_a_up[l], rwkv_g_up=rwkv_g_up[l],
              rwkv_k_k=rwkv_k_k[l], rwkv_k_a=rwkv_k_a[l], rwkv_r_k=rwkv_r_k[l], rwkv_gn_w=rwkv_gn_w[l],
              rwkv_gn_b=rwkv_gn_b[l], w_rwkv_out=w_rwkv_out[l], q_norm=q_norm[l], k_norm=k_norm[l],
              cmp_pe=cmp_pe[l], cmp_w1=cmp_w1[l], cmp_w2=cmp_w2[l], w_nsa_out=w_nsa_out[l], w_out=w_out[l],
              ffn_norm=ffn_norm[l], peer_wq=peer_wq[l], peer_keys=peer_keys[l], peer_u=peer_u[l],
              peer_v=peer_v[l], ple_norm=ple_norm[l], ple_w_gate=ple_w_gate[l], ple_w_in=ple_w_in[l],
              peer_u_packed=pack_expert_table(peer_u[l]), peer_v_packed=pack_expert_table(peer_v[l]))
    attend_p = functools.partial(nsa_prompt, lp=lp)
    xp, kvc_p, kvs_p, kvw_p, wkv_p, sh_p = layer_forward(
        xp, p_prompt[l], pos_p, jnp.zeros((b_p, RWKV_COLS), xp.dtype),
        jnp.zeros((b_p, RWKV_HEADS, HEAD_DIM, HEAD_DIM), jnp.float32), attend_p, lp)
    win_prev = state_win_kv[l]
    attend_s = functools.partial(nsa_sample, cache_cmp=cache_cmp_kv[l], cache_sel=cache_sel_kv[l],
                                 page_table=page_table, state_win=win_prev, lp=lp)
    xs, kvc_s, kvs_s, kvw_s, wkv_s, sh_s = layer_forward(xs, p_sample[l], pos_s, state_shift[l], state_wkv[l], attend_s, lp)
    win_s = jnp.concatenate([win_prev.astype(kvw_s.dtype), kvw_s], 1)[:, -win_prev.shape[1]:]
    st = lambda a: a[None]
    return (xp, xs, st(kvc_p), st(kvc_s), st(kvs_p), st(kvs_s),
            st(kvw_p[:, -min(WINDOW, s_p):]), st(win_s), st(wkv_p), st(wkv_s), st(sh_p), st(sh_s))
```
